```python
import jax
import jax.numpy as jnp
from jax import lax
import numpy as np

D_MODEL = 2048
BATCH = 1
SEQ = 8192
DEPTH = 4
DEC_BATCH = 8
DEC_SEQ = 32
PAST_LEN = 1024

CHUNK = 64
N_MIXERS = 3
N_CONV = (DEPTH + 2) // 3
N_HGRN = (DEPTH + 1) // 3
N_SWA = DEPTH // 3
CONV_WIDTH = 31
CONV_STATE = CONV_WIDTH - 1
HGRN_EXPAND = 128
HGRN_HEADS = D_MODEL // HGRN_EXPAND
HGRN_DK = HGRN_EXPAND
HGRN_DV = D_MODEL // HGRN_HEADS
SWA_HEAD_DIM = 64
SWA_HEADS = D_MODEL // SWA_HEAD_DIM
SWA_KV_HEADS = 4
SWA_GROUP = SWA_HEADS // SWA_KV_HEADS
WINDOW = 128
WINDOW_CHUNKS = -(-WINDOW // CHUNK)
MEM_TOKENS = 256
MEM_HEADS = 4
MEM_HEAD_DIM = D_MODEL // MEM_HEADS
N_EXPERTS = 32
TOP_K = 4
D_FF = D_MODEL
SWIGLU_LIMIT = 7.0
SWIGLU_ALPHA = 1.702
MOE_BLOCK = 128
LN_EPS = 1e-5
DEEPNORM_ALPHA = (2 * DEPTH) ** 0.25
DEEPNORM_BETA = (8 * DEPTH) ** -0.25

kernel_name = 'hybrid_streaming_encoder_step'


def _normal(key, shape, scale):
    return jax.random.normal(key, shape, jnp.float32) * scale


def _layer_norm(x, g, b):
    xf = x.astype(jnp.float32)
    mu = jnp.mean(xf, axis=-1, keepdims=True)
    var = jnp.mean(jnp.square(xf - mu), axis=-1, keepdims=True)
    y = (xf - mu) * lax.rsqrt(var + LN_EPS) * g.astype(jnp.float32) + b.astype(jnp.float32)
    return y.astype(x.dtype)


def _conv_module(x, hist, w_in, b_in, w_dw, b_dw, n_g, n_b, w_out, b_out):
    h = x @ w_in + b_in
    u = h[..., :D_MODEL] * jax.nn.sigmoid(h[..., D_MODEL:])
    up = jnp.concatenate([hist.astype(u.dtype), u], axis=1)
    c = lax.conv_general_dilated(up, w_dw[:, None, :].astype(u.dtype), window_strides=(1,),
                                 padding='VALID', dimension_numbers=('NWC', 'WIO', 'NWC'),
                                 feature_group_count=D_MODEL) + b_dw
    c = jax.nn.silu(_layer_norm(c, n_g, n_b))
    return c @ w_out + b_out, up[:, -CONV_STATE:]


def _gated_recurrence(q, k, v, logf, s0):
    bsz, t_len, n_h, _ = q.shape
    dv = v.shape[-1]
    blk = t_len if t_len <= CHUNK else CHUNK
    n_blk = t_len // blk

    def to_blocks(a):
        return a.reshape(bsz, n_blk, blk, n_h, a.shape[-1]).transpose(1, 0, 3, 2, 4)

    causal = jnp.tril(jnp.ones((blk, blk), dtype=bool))[:, :, None]

    def step(s, inp):
        qb, kb, vb, gb = inp
        b = jnp.cumsum(gb, axis=2)
        inter = jnp.einsum('bhtk,bhkv->bhtv', qb * jnp.exp(b), s)
        diff = b[:, :, :, None, :] - b[:, :, None, :, :]
        decay = jnp.exp(jnp.where(causal, diff, -jnp.inf))
        scores = jnp.einsum('bhtk,bhsk,bhtsk->bhts', qb, kb, decay)
        intra = jnp.einsum('bhts,bhsv->bhtv', scores, vb)
        b_last = b[:, :, -1:, :]
        s_new = (jnp.exp(b_last[:, :, 0, :])[..., None] * s
                 + jnp.einsum('bhsk,bhsv->bhkv', kb * jnp.exp(b_last - b), vb))
        return s_new, inter + intra

    s_fin, o = lax.scan(step, s0, (to_blocks(q), to_blocks(k), to_blocks(v), to_blocks(logf)))
    o = o.transpose(1, 0, 3, 2, 4).reshape(bsz, t_len, n_h, dv)
    return o, s_fin


def _hgrn2(x, s0, lower_bound, w_in, norm_g, w_out):
    bsz, t_len, _ = x.shape
    q, f, i, og = jnp.split((x @ w_in).astype(jnp.float32), 4, axis=-1)

    def heads(a, d):
        return a.reshape(bsz, t_len, HGRN_HEADS, d)

    forget = lower_bound + (1.0 - lower_bound) * jax.nn.sigmoid(f)
    o, s_fin = _gated_recurrence(heads(jax.nn.silu(q), HGRN_DK), heads(1.0 - forget, HGRN_DK),
                                 heads(i, HGRN_DV), heads(jnp.log(forget), HGRN_DK),
                                 s0.astype(jnp.float32))
    o = o * lax.rsqrt(jnp.mean(jnp.square(o), axis=-1, keepdims=True) + LN_EPS)
    o = o.reshape(bsz, t_len, D_MODEL) * norm_g.astype(jnp.float32) * jax.nn.sigmoid(og)
    return o.astype(x.dtype) @ w_out, s_fin.astype(x.dtype)


def _swa_project(x, w_qkv, b_qkv):
    bsz, t_len, _ = x.shape
    h = x @ w_qkv + b_qkv
    qw = SWA_HEADS * SWA_HEAD_DIM
    kw = SWA_KV_HEADS * SWA_HEAD_DIM
    q = h[..., :qw].reshape(bsz, t_len, SWA_KV_HEADS, SWA_GROUP, SWA_HEAD_DIM)
    k = h[..., qw:qw + kw].reshape(bsz, t_len, SWA_KV_HEADS, SWA_HEAD_DIM)
    v = h[..., qw + kw:].reshape(bsz, t_len, SWA_KV_HEADS, SWA_HEAD_DIM)
    return q, k, v


def _alibi_chunk_bias(qpos, kpos):
    qp = qpos[..., :, None]
    kp = kpos[..., None, :]
    qc = qp // CHUNK
    kc = jnp.where(kp >= 0, kp, 0) // CHUNK
    visible = (kp >= 0) & (kc <= qc) & (qc - kc <= WINDOW_CHUNKS)
    slopes = (2.0 ** (-8.0 * jnp.arange(1, SWA_HEADS + 1, dtype=jnp.float32) / SWA_HEADS)
              ).reshape(SWA_KV_HEADS, SWA_GROUP)
    dist = jnp.abs(qp - kp).astype(jnp.float32)[..., None, None, :, :]
    bias = -slopes[:, :, None, None] * dist
    return jnp.where(visible[..., None, None, :, :], bias, -jnp.inf)


def _sink_attend(q, k, v, bias, sinks):
    s = jnp.einsum('...qkgd,...skd->...kgqs', q, k).astype(jnp.float32) * (SWA_HEAD_DIM ** -0.5) + bias
    sink = jnp.broadcast_to(sinks.astype(jnp.float32).reshape(SWA_KV_HEADS, SWA_GROUP, 1, 1),
                            s.shape[:-1] + (1,))
    w = jax.nn.softmax(jnp.concatenate([s, sink], axis=-1), axis=-1)[..., :-1]
    return jnp.einsum('...kgqs,...skd->...qkgd', w.astype(v.dtype), v)


def _swa_prompt(q, k, v, sinks):
    bsz, t_len = q.shape[:2]
    n_c = t_len // CHUNK
    pad = WINDOW_CHUNKS * CHUNK

    def band(a):
        ap = jnp.pad(a, ((0, 0), (pad, 0), (0, 0), (0, 0))).reshape(
            bsz, n_c + WINDOW_CHUNKS, CHUNK, *a.shape[2:])
        return jnp.concatenate([ap[:, j:j + n_c] for j in range(WINDOW_CHUNKS + 1)], axis=2)

    c_idx = jnp.arange(n_c, dtype=jnp.int32)[:, None] * CHUNK
    qpos = c_idx + jnp.arange(CHUNK, dtype=jnp.int32)[None, :]
    kpos = c_idx - pad + jnp.arange(pad + CHUNK, dtype=jnp.int32)[None, :]
    qb = q.reshape(bsz, n_c, CHUNK, SWA_KV_HEADS, SWA_GROUP, SWA_HEAD_DIM)
    o = _sink_attend(qb, band(k), band(v), _alibi_chunk_bias(qpos, kpos), sinks)
    return o.reshape(bsz, t_len, SWA_HEADS * SWA_HEAD_DIM)


def _swa_sample(q, k, v, cache_k, cache_v, sinks):
    bsz, s_len = q.shape[:2]
    keep = cache_k.shape[1]
    kk = jnp.concatenate([cache_k.astype(k.dtype), k], axis=1)
    vv = jnp.concatenate([cache_v.astype(v.dtype), v], axis=1)
    qpos = PAST_LEN + jnp.arange(s_len, dtype=jnp.int32)
    kpos = jnp.concatenate([PAST_LEN - keep + jnp.arange(keep, dtype=jnp.int32), qpos])
    o = _sink_attend(q, kk, vv, _alibi_chunk_bias(qpos, kpos), sinks)
    return o.reshape(bsz, s_len, SWA_HEADS * SWA_HEAD_DIM), kk[:, -keep:], vv[:, -keep:]


def _mem_kv(mem, w_kv):
    bsz, m_len, _ = mem.shape
    k, v = jnp.split(mem @ w_kv, 2, axis=-1)
    return (k.reshape(bsz, m_len, MEM_HEADS, MEM_HEAD_DIM),
            v.reshape(bsz, m_len, MEM_HEADS, MEM_HEAD_DIM))


def _mem_attend(x, mem_k, mem_v, w_q, w_out):
    bsz, t_len, _ = x.shape
    q = (x @ w_q).reshape(bsz, t_len, MEM_HEADS, MEM_HEAD_DIM)
    s = jnp.einsum('bthd,bmhd->bhtm', q, mem_k.astype(q.dtype)).astype(jnp.float32) * (MEM_HEAD_DIM ** -0.5)
    w = jax.nn.softmax(s, axis=-1)
    o = jnp.einsum('bhtm,bmhd->bthd', w.astype(x.dtype), mem_v.astype(x.dtype))
    return o.reshape(bsz, t_len, D_MODEL) @ w_out


def _moe(x, w_r, b_r, w_gu, b_gu, w_d, b_d):
    n = x.shape[0]
    n_assign = n * TOP_K
    logits = (x @ w_r).astype(jnp.float32) + b_r.astype(jnp.float32)
    top_val, top_idx = lax.top_k(logits, TOP_K)
    gates = jax.nn.softmax(top_val, axis=-1)
    flat_e = top_idx.reshape(-1)
    order = jnp.argsort(flat_e)
    sorted_e = flat_e[order]
    counts = jnp.bincount(flat_e, length=N_EXPERTS)
    padded = (counts + MOE_BLOCK - 1) // MOE_BLOCK * MOE_BLOCK
    pad_end = jnp.cumsum(padded)
    pad_start = pad_end - padded
    grp_start = jnp.cumsum(counts) - counts
    dest = pad_start[sorted_e] + jnp.arange(n_assign) - grp_start[sorted_e]
    n_rows = (-(-n_assign // MOE_BLOCK) + N_EXPERTS) * MOE_BLOCK
    buf = jnp.zeros((n_rows, D_MODEL), x.dtype).at[dest].set(x[order // TOP_K])
    n_blocks = n_rows // MOE_BLOCK
    block_e = jnp.minimum(jnp.searchsorted(pad_end, jnp.arange(n_blocks) * MOE_BLOCK, side='right'),
                          N_EXPERTS - 1)

    def expert_block(args):
        xb, e = args
        h = xb @ w_gu[e] + b_gu[e]
        gate = jnp.minimum(h[:, :D_FF], SWIGLU_LIMIT)
        lin = jnp.clip(h[:, D_FF:], -SWIGLU_LIMIT, SWIGLU_LIMIT)
        act = gate * jax.nn.sigmoid(SWIGLU_ALPHA * gate) * (lin + 1.0)
        return act @ w_d[e] + b_d[e]

    out = lax.map(expert_block, (buf.reshape(n_blocks, MOE_BLOCK, D_MODEL), block_e))
    out = out.reshape(n_rows, D_MODEL)
    per_assign = jnp.zeros((n_assign, D_MODEL), out.dtype).at[order].set(out[dest])
    return jnp.einsum('nkd,nk->nd', per_assign.reshape(n, TOP_K, D_MODEL), gates.astype(out.dtype))


def setup_inputs(seed: int = 0) -> dict:
    key = jax.random.key(seed)
    ks = iter(jax.random.split(key, 48))
    d = D_MODEL
    inv = d ** -0.5
    keep = min(WINDOW_CHUNKS * CHUNK, PAST_LEN)
    qkv_w = (SWA_HEADS + 2 * SWA_KV_HEADS) * SWA_HEAD_DIM
    return {
        'x_prompt': _normal(next(ks), (BATCH, SEQ, d), 1.0),
        'x_sample': _normal(next(ks), (DEC_BATCH, DEC_SEQ, d), 1.0),
        'mem_prompt': _normal(next(ks), (BATCH, MEM_TOKENS, d), 1.0),
        'state_conv': _normal(next(ks), (N_CONV, DEC_BATCH, CONV_STATE, d), 0.5),
        'state_hgrn': _normal(next(ks), (N_HGRN, DEC_BATCH, HGRN_HEADS, HGRN_DK, HGRN_DV), 0.5),
        'cache_swa_k': _normal(next(ks), (N_SWA, DEC_BATCH, keep, SWA_KV_HEADS, SWA_HEAD_DIM), 1.0),
        'cache_swa_v': _normal(next(ks), (N_SWA, DEC_BATCH, keep, SWA_KV_HEADS, SWA_HEAD_DIM), 1.0),
        'cache_mem_k': _normal(next(ks), (DEPTH, DEC_BATCH, MEM_TOKENS, MEM_HEADS, MEM_HEAD_DIM), 1.0),
        'cache_mem_v': _normal(next(ks), (DEPTH, DEC_BATCH, MEM_TOKENS, MEM_HEADS, MEM_HEAD_DIM), 1.0),
        'ln_g': 1.0 + _normal(next(ks), (DEPTH, 3, d), 0.02),
        'ln_b': _normal(next(ks), (DEPTH, 3, d), 0.02),
        'conv_w_in': _normal(next(ks), (N_CONV, d, 2 * d), inv),
        'conv_b_in': _normal(next(ks), (N_CONV, 2 * d), 0.02),
        'conv_w_dw': _normal(next(ks), (N_CONV, CONV_WIDTH, d), CONV_WIDTH ** -0.5),
        'conv_b_dw': _normal(next(ks), (N_CONV, d), 0.02),
        'conv_ln_g': 1.0 + _normal(next(ks), (N_CONV, d), 0.02),
        'conv_ln_b': _normal(next(ks), (N_CONV, d), 0.02),
        'conv_w_out': _normal(next(ks), (N_CONV, d, d), inv * DEEPNORM_BETA),
        'conv_b_out': _normal(next(ks), (N_CONV, d), 0.02),
        'hgrn_w_in': _normal(next(ks), (N_HGRN, d, 4 * d), inv),
        'hgrn_lb': _normal(next(ks), (DEPTH, HGRN_HEADS * HGRN_DK), 0.5),
        'hgrn_norm_g': 1.0 + _normal(next(ks), (N_HGRN, d), 0.02),
        'hgrn_w_out': _normal(next(ks), (N_HGRN, d, d), inv * DEEPNORM_BETA),
        'swa_w_qkv': _normal(next(ks), (N_SWA, d, qkv_w), inv),
        'swa_b_qkv': _normal(next(ks), (N_SWA, qkv_w), 0.02),
        'swa_sinks': _normal(next(ks), (N_SWA, SWA_HEADS), 1.0),
        'swa_w_out': _normal(next(ks), (N_SWA, SWA_HEADS * SWA_HEAD_DIM, d),
                             (SWA_HEADS * SWA_HEAD_DIM) ** -0.5 * DEEPNORM_BETA),
        'swa_b_out': _normal(next(ks), (N_SWA, d), 0.02),
        'mem_w_q': _normal(next(ks), (DEPTH, d, d), inv),
        'mem_w_kv': _normal(next(ks), (DEPTH, d, 2 * d), inv),
        'mem_w_out': _normal(next(ks), (DEPTH, d, d), inv * DEEPNORM_BETA),
        'moe_w_router': _normal(next(ks), (DEPTH, d, N_EXPERTS), inv),
        'moe_b_router': _normal(next(ks), (DEPTH, N_EXPERTS), 0.01),
        'moe_w_gate_up': _normal(next(ks), (DEPTH, N_EXPERTS, d, 2 * D_FF), inv),
        'moe_b_gate_up': _normal(next(ks), (DEPTH, N_EXPERTS, 2 * D_FF), 0.02),
        'moe_w_down': _normal(next(ks), (DEPTH, N_EXPERTS, D_FF, d), D_FF ** -0.5 * DEEPNORM_BETA),
        'moe_b_down': _normal(next(ks), (DEPTH, N_EXPERTS, d), 0.02),
    }


def reference(x_prompt, x_sample, mem_prompt, state_conv, state_hgrn, cache_swa_k, cache_swa_v,
              cache_mem_k, cache_mem_v, ln_g, ln_b, conv_w_in, conv_b_in, conv_w_dw, conv_b_dw,
              conv_ln_g, conv_ln_b, conv_w_out, conv_b_out, hgrn_w_in, hgrn_lb, hgrn_norm_g,
              hgrn_w_out, swa_w_qkv, swa_b_qkv, swa_sinks, swa_w_out, swa_b_out, mem_w_q, mem_w_kv,
              mem_w_out, moe_w_router, moe_b_router, moe_w_gate_up, moe_b_gate_up, moe_w_down,
              moe_b_down):
    lb_p = jax.nn.softmax(hgrn_lb.astype(jnp.float32), axis=0)
    lower_bounds = jnp.cumsum(lb_p, axis=0) - lb_p[0]
    keep = min(WINDOW_CHUNKS * CHUNK, PAST_LEN)

    def trunk(x, conv_hist, hgrn_s, swa_k, swa_v, mem_k, mem_v):
        new_conv, new_hgrn, new_k, new_v = [], [], [], []
        for i in range(DEPTH):
            kind, slot = i % N_MIXERS, i // N_MIXERS
            if kind == 0:
                h, st = _conv_module(x, conv_hist[slot], conv_w_in[slot], conv_b_in[slot],
                                     conv_w_dw[slot], conv_b_dw[slot], conv_ln_g[slot],
                                     conv_ln_b[slot], conv_w_out[slot], conv_b_out[slot])
                new_conv.append(st)
            elif kind == 1:
                h, st = _hgrn2(x, hgrn_s[slot], lower_bounds[i], hgrn_w_in[slot],
                               hgrn_norm_g[slot], hgrn_w_out[slot])
                new_hgrn.append(st)
            else:
                q, k, v = _swa_project(x, swa_w_qkv[slot], swa_b_qkv[slot])
                if swa_k is None:
                    o = _swa_prompt(q, k, v, swa_sinks[slot])
                    kk, vv = k[:, -keep:], v[:, -keep:]
                else:
                    o, kk, vv = _swa_sample(q, k, v, swa_k[slot], swa_v[slot], swa_sinks[slot])
                h = o @ swa_w_out[slot] + swa_b_out[slot]
                new_k.append(kk)
                new_v.append(vv)
            x = _layer_norm(DEEPNORM_ALPHA * x + h, ln_g[i, 0], ln_b[i, 0])
            c = _mem_attend(x, mem_k[i], mem_v[i], mem_w_q[i], mem_w_out[i])
            x = _layer_norm(DEEPNORM_ALPHA * x + c, ln_g[i, 1], ln_b[i, 1])
            bsz, t_len, _ = x.shape
            ff = _moe(x.reshape(bsz * t_len, D_MODEL), moe_w_router[i], moe_b_router[i],
                      moe_w_gate_up[i], moe_b_gate_up[i], moe_w_down[i], moe_b_down[i])
            x = _layer_norm(DEEPNORM_ALPHA * x + ff.reshape(bsz, t_len, D_MODEL), ln_g[i, 2], ln_b[i, 2])
        return x, jnp.stack(new_conv), jnp.stack(new_hgrn), jnp.stack(new_k), jnp.stack(new_v)

    bsz_p = x_prompt.shape[0]
    mem_kv = [_mem_kv(mem_prompt, mem_w_kv[i]) for i in range(DEPTH)]
    p_mem_k = jnp.stack([kv[0] for kv in mem_kv])
    p_mem_v = jnp.stack([kv[1] for kv in mem_kv])
    conv0 = jnp.zeros((N_CONV, bsz_p, CONV_STATE, D_MODEL), x_prompt.dtype)
    hgrn0 = jnp.zeros((N_HGRN, bsz_p, HGRN_HEADS, HGRN_DK, HGRN_DV), jnp.float32)
    y_prompt, p_conv, p_hgrn, p_swa_k, p_swa_v = trunk(x_prompt, conv0, hgrn0, None, None,
                                                       p_mem_k, p_mem_v)
    y_sample, s_conv, s_hgrn, s_swa_k, s_swa_v = trunk(x_sample, state_conv, state_hgrn,
                                                       cache_swa_k, cache_swa_v,
                                                       cache_mem_k, cache_mem_v)
    return (y_prompt, y_sample, p_conv, p_hgrn, p_swa_k, p_swa_v, p_mem_k, p_mem_v,
            s_conv, s_hgrn, s_swa_k, s_swa_v)
```

```python
import functools

import jax
import jax.numpy as jnp
from jax import lax
from jax.experimental import pallas as pl
from jax.experimental.pallas import tpu as pltpu

PAST_LEN = 1024
CHUNK = 64
WINDOW_CHUNKS = 2
SWA_HEAD_DIM = 64
HGRN_HEAD_DIM = 128
TOP_K = 4
SWIGLU_LIMIT = 7.0
SWIGLU_ALPHA = 1.702
LN_EPS = 1e-5

LANE = 128
SUBLANE = 8
VMEM_LIMIT = 56 * 1024 * 1024
HIST_ROWS = 32
MOE_BLK = 256
MOE_SUPER = 1024
MOE_FT = 256
HGRN_SUB = 16
NEG = -1e30

_bf16 = jnp.bfloat16
_f32 = jnp.float32


def _params(*sem):
    return pltpu.CompilerParams(dimension_semantics=sem, vmem_limit_bytes=VMEM_LIMIT)


def _row_tile(n, cap=256):
    for t in (cap, 128, 64, 32, 16, 8):
        if t <= cap and n % t == 0:
            return t
    raise ValueError(f"row count {n} is not a multiple of {SUBLANE}")


def _ln(y, g, b):
    mu = jnp.mean(y, axis=-1, keepdims=True)
    d = y - mu
    var = jnp.mean(d * d, axis=-1, keepdims=True)
    return d * lax.rsqrt(var + LN_EPS) * g + b


def _dot(a, b):
    return jnp.dot(a, b, preferred_element_type=_f32)


def _dot_nt(a, b):
    return lax.dot_general(a, b, (((1,), (1,)), ((), ())), preferred_element_type=_f32)


def _mm_kernel(x_ref, w_ref, b_ref, o_ref):
    o_ref[...] = _dot(x_ref[...].astype(_bf16), w_ref[...]) + b_ref[...]


def _mm(x, w, b, *, tn=2048):
    m, k = x.shape
    n = w.shape[1]
    tn = min(tn, n)
    tm = _row_tile(m)
    return pl.pallas_call(
        _mm_kernel,
        grid=(n // tn, m // tm),
        in_specs=[pl.BlockSpec((tm, k), lambda j, i: (i, 0)),
                  pl.BlockSpec((k, tn), lambda j, i: (0, j)),
                  pl.BlockSpec((1, tn), lambda j, i: (0, j))],
        out_specs=pl.BlockSpec((tm, tn), lambda j, i: (i, j)),
        out_shape=jax.ShapeDtypeStruct((m, n), _f32),
        compiler_params=_params("arbitrary", "arbitrary"),
        name="mm_bias",
    )(x, w, b)


def _mm_glu_kernel(x_ref, wa_ref, wb_ref, ba_ref, bb_ref, o_ref):
    x = x_ref[...].astype(_bf16)
    a = _dot(x, wa_ref[...]) + ba_ref[...]
    g = _dot(x, wb_ref[...]) + bb_ref[...]
    o_ref[...] = a * jax.nn.sigmoid(g)


def _mm_glu(x, w, b, *, tn=1024):
    m, k = x.shape
    n = w.shape[1] // 2
    tn = min(tn, n)
    tm = _row_tile(m)
    nj = n // tn
    return pl.pallas_call(
        _mm_glu_kernel,
        grid=(nj, m // tm),
        in_specs=[pl.BlockSpec((tm, k), lambda j, i: (i, 0)),
                  pl.BlockSpec((k, tn), lambda j, i: (0, j)),
                  pl.BlockSpec((k, tn), lambda j, i: (0, j + nj)),
                  pl.BlockSpec((1, tn), lambda j, i: (0, j)),
                  pl.BlockSpec((1, tn), lambda j, i: (0, j + nj))],
        out_specs=pl.BlockSpec((tm, tn), lambda j, i: (i, j)),
        out_shape=jax.ShapeDtypeStruct((m, n), _f32),
        compiler_params=_params("arbitrary", "arbitrary"),
        name="mm_glu",
    )(x, w, w, b, b)


def _mm_res_ln_kernel(x_ref, w_ref, b_ref, res_ref, g_ref, be_ref, o_ref, *, alpha):
    h = _dot(x_ref[...].astype(_bf16), w_ref[...]) + b_ref[...]
    o_ref[...] = _ln(alpha * res_ref[...] + h, g_ref[...], be_ref[...])


def _mm_norm_res_ln_kernel(x_ref, ng_ref, nb_ref, w_ref, b_ref, res_ref, g_ref, be_ref, o_ref, *, alpha):
    c = _ln(x_ref[...], ng_ref[...], nb_ref[...])
    c = c * jax.nn.sigmoid(c)
    h = _dot(c.astype(_bf16), w_ref[...]) + b_ref[...]
    o_ref[...] = _ln(alpha * res_ref[...] + h, g_ref[...], be_ref[...])


def _mm_res_ln(x, w, b, res, g, be, *, alpha, pre_norm=None):
    m, k = x.shape
    n = w.shape[1]
    tm = _row_tile(m)
    row = lambda i: (i, 0)
    fix = lambda i: (0, 0)
    specs = [pl.BlockSpec((tm, k), row)]
    args = [x]
    if pre_norm is not None:
        specs += [pl.BlockSpec((1, k), fix), pl.BlockSpec((1, k), fix)]
        args += list(pre_norm)
        body = _mm_norm_res_ln_kernel
    else:
        body = _mm_res_ln_kernel
    specs += [pl.BlockSpec((k, n), fix), pl.BlockSpec((1, n), fix), pl.BlockSpec((tm, n), row),
              pl.BlockSpec((1, n), fix), pl.BlockSpec((1, n), fix)]
    args += [w, b, res, g, be]
    return pl.pallas_call(
        functools.partial(body, alpha=alpha),
        grid=(m // tm,),
        in_specs=specs,
        out_specs=pl.BlockSpec((tm, n), row),
        out_shape=jax.ShapeDtypeStruct((m, n), _f32),
        compiler_params=_params("arbitrary"),
        name="mm_res_ln",
    )(*args)


def _conv_kernel(hist_ref, u_ref, w_ref, b_ref, o_ref, buf_ref, *, zero_first, tt, lb, kw):
    hist = hist_ref[...]
    if zero_first:
        hist = jnp.where(pl.program_id(0) == 0, 0.0, hist)
    buf_ref[0:HIST_ROWS, :] = hist
    buf_ref[HIST_ROWS:, :] = u_ref[...]
    off = HIST_ROWS - (kw - 1)
    rb = min(tt, 64)
    for r in range(tt // rb):
        for c in range(lb // LANE):
            cs = slice(c * LANE, (c + 1) * LANE)
            acc = jnp.zeros((rb, LANE), _f32)
            for j in range(kw):
                acc = acc + w_ref[j:j + 1, cs] * buf_ref[r * rb + off + j:r * rb + off + j + rb, cs]
            o_ref[r * rb:(r + 1) * rb, cs] = acc + b_ref[:, cs]


def _conv(u, hist_s, w, b, *, t_prompt, n_batch_s, t_s):
    n, d = u.shape
    kw = w.shape[0]
    lb = min(512, d)
    tt = _row_tile(t_prompt)
    hb = tt // HIST_ROWS
    common = dict(out_shape=jax.ShapeDtypeStruct((n, d), _f32), compiler_params=_params("arbitrary", "arbitrary"))
    c = pl.pallas_call(
        functools.partial(_conv_kernel, zero_first=True, tt=tt, lb=lb, kw=kw),
        grid=(t_prompt // tt, d // lb),
        in_specs=[pl.BlockSpec((HIST_ROWS, lb), lambda i, j: (jnp.maximum(i * hb - 1, 0), j)),
                  pl.BlockSpec((tt, lb), lambda i, j: (i, j)),
                  pl.BlockSpec((kw, lb), lambda i, j: (0, j)),
                  pl.BlockSpec((1, lb), lambda i, j: (0, j))],
        out_specs=pl.BlockSpec((tt, lb), lambda i, j: (i, j)),
        scratch_shapes=[pltpu.VMEM((HIST_ROWS + tt, lb), _f32)],
        name="conv_prompt", **common,
    )(u, u, w, b)
    ob = t_prompt // t_s
    return pl.pallas_call(
        lambda full_ref, *refs: _conv_kernel(*refs, zero_first=False, tt=t_s, lb=lb, kw=kw),
        grid=(n_batch_s, d // lb),
        in_specs=[pl.BlockSpec(memory_space=pl.ANY),
                  pl.BlockSpec((None, HIST_ROWS, lb), lambda i, j: (i, 0, j)),
                  pl.BlockSpec((t_s, lb), lambda i, j: (ob + i, j)),
                  pl.BlockSpec((kw, lb), lambda i, j: (0, j)),
                  pl.BlockSpec((1, lb), lambda i, j: (0, j))],
        out_specs=pl.BlockSpec((t_s, lb), lambda i, j: (ob + i, j)),
        scratch_shapes=[pltpu.VMEM((HIST_ROWS + t_s, lb), _f32)],
        input_output_aliases={0: 0},
        name="conv_sample", **common,
    )(c, hist_s, u, w, b)


def _hgrn_kernel(q_ref, f_ref, i_ref, og_ref, lb_ref, ng_ref, s0_ref, o_ref, sf_ref, st_ref, *, tt, blk):
    t = pl.program_id(2)

    @pl.when(t == 0)
    def _():
        st_ref[...] = s0_ref[...]

    lbv = lb_ref[...]
    ngv = ng_ref[...]
    rr = lax.broadcasted_iota(jnp.int32, (blk, blk), 0)
    cc = lax.broadcasted_iota(jnp.int32, (blk, blk), 1)
    tril = (cc <= rr).astype(_f32)
    sub_row = lax.broadcasted_iota(jnp.int32, (HGRN_SUB, LANE), 0)

    def chunk(c, carry):
        base = pl.multiple_of(c * blk, blk)
        rows = pl.ds(base, blk)
        q = q_ref[rows, :]
        qh = q * jax.nn.sigmoid(q)
        forget = lbv + (1.0 - lbv) * jax.nn.sigmoid(f_ref[rows, :])
        kh = 1.0 - forget
        v = i_ref[rows, :]
        b = jnp.dot(tril, jnp.log(forget), precision=lax.Precision.HIGHEST, preferred_element_type=_f32)
        st = st_ref[...]
        inter = _dot_nt((qh * jnp.exp(b)).astype(_bf16), st.astype(_bf16))
        b_last = b[blk - 1:blk, :]
        kf_end = (kh * jnp.exp(b_last - b)).astype(_bf16)
        st_ref[...] = st * jnp.exp(b_last) + lax.dot_general(
            v.astype(_bf16), kf_end, (((0,), (0,)), ((), ())), preferred_element_type=_f32)
        vb = v.astype(_bf16)
        for blk_i in range(blk // HGRN_SUB):
            r0 = blk_i * HGRN_SUB
            b_i = b[r0:r0 + HGRN_SUB, :]
            q_i = qh[r0:r0 + HGRN_SUB, :]
            acc = inter[r0:r0 + HGRN_SUB, :]
            for s in range(HGRN_SUB):
                keep = sub_row >= s
                decay = jnp.exp(jnp.where(keep, b_i - b[r0 + s:r0 + s + 1, :], NEG))
                col = jnp.sum(q_i * decay * kh[r0 + s:r0 + s + 1, :], axis=-1, keepdims=True)
                acc = acc + col * v[r0 + s:r0 + s + 1, :]
            if r0 > 0:
                anchor = b[r0 - 1:r0, :]
                qf = (q_i * jnp.exp(b_i - anchor)).astype(_bf16)
                kf = (kh[0:r0, :] * jnp.exp(anchor - b[0:r0, :])).astype(_bf16)
                acc = acc + _dot(_dot_nt(qf, kf).astype(_bf16), vb[0:r0, :])
            o = acc * lax.rsqrt(jnp.mean(acc * acc, axis=-1, keepdims=True) + LN_EPS)
            o_rows = pl.ds(base + r0, HGRN_SUB)
            o_ref[o_rows, :] = o * ngv * jax.nn.sigmoid(og_ref[o_rows, :])
        return carry

    lax.fori_loop(0, tt // blk, chunk, 0)

    @pl.when(t == pl.num_programs(2) - 1)
    def _():
        sf_ref[...] = st_ref[...]


def _hgrn(proj, lb, ng, s0_t, o_prev, *, row_off, n_batch, t_len, name):
    n, d4 = proj.shape
    d = d4 // 4
    nh = d // HGRN_HEAD_DIM
    blk = min(t_len, CHUNK)
    tt = _row_tile(t_len)
    nt = t_len // tt
    ob = row_off // tt
    hd = HGRN_HEAD_DIM
    row = lambda k: (lambda b, h, t: (ob + b * nt + t, k * nh + h))
    head = lambda b, h, t: (0, h)
    in_specs = [pl.BlockSpec((tt, hd), row(0)), pl.BlockSpec((tt, hd), row(1)),
                pl.BlockSpec((tt, hd), row(2)), pl.BlockSpec((tt, hd), row(3)),
                pl.BlockSpec((1, hd), head), pl.BlockSpec((1, hd), head),
                pl.BlockSpec((None, None, hd, hd), lambda b, h, t: (b, h, 0, 0))]
    args = [proj, proj, proj, proj, lb, ng, s0_t]
    body = functools.partial(_hgrn_kernel, tt=tt, blk=blk)
    aliases = {}
    if o_prev is not None:
        in_specs = [pl.BlockSpec(memory_space=pl.ANY)] + in_specs
        args = [o_prev] + args
        inner = body
        body = lambda full_ref, *refs: inner(*refs)
        aliases = {0: 0}
    return pl.pallas_call(
        body,
        grid=(n_batch, nh, nt),
        in_specs=in_specs,
        out_specs=[pl.BlockSpec((tt, hd), lambda b, h, t: (ob + b * nt + t, h)),
                   pl.BlockSpec((None, None, hd, hd), lambda b, h, t: (b, h, 0, 0))],
        out_shape=[jax.ShapeDtypeStruct((n, d), _f32),
                   jax.ShapeDtypeStruct((n_batch, nh, hd, hd), _f32)],
        scratch_shapes=[pltpu.VMEM((hd, hd), _f32)],
        input_output_aliases=aliases,
        compiler_params=_params("arbitrary", "arbitrary", "arbitrary"),
        name=name,
    )(*args)


def _swa_kernel(sink_ref, q_ref, kc_ref, vc_ref, kp_ref, vp_ref, o_ref, *, tq, win, pos0, n_kv, group):
    dh = SWA_HEAD_DIM
    qpos0 = pos0 + pl.program_id(1) * tq
    n_heads = n_kv * group

    def bias_terms(nk, kpos0):
        qp = qpos0 + lax.broadcasted_iota(jnp.int32, (tq, nk), 0)
        kp = kpos0 + lax.broadcasted_iota(jnp.int32, (tq, nk), 1)
        qc = qp // CHUNK
        kc = jnp.maximum(kp, 0) // CHUNK
        vis = (kp >= 0) & (kc <= qc) & (qc - kc <= WINDOW_CHUNKS)
        return vis, jnp.abs(qp - kp).astype(_f32)

    vis_p, dist_p = bias_terms(win, qpos0 - win)
    vis_c, dist_c = bias_terms(tq, qpos0)
    scale = dh ** -0.5
    for kv in range(n_kv):
        ks = slice(kv * dh, (kv + 1) * dh)
        kp = kp_ref[:, ks].astype(_bf16)
        kc = kc_ref[:, ks].astype(_bf16)
        vp = vp_ref[:, ks].astype(_bf16)
        vc = vc_ref[:, ks].astype(_bf16)
        for g in range(group):
            h = kv * group + g
            slope = 2.0 ** (-8.0 * (h + 1) / n_heads)
            hs = slice(h * dh, (h + 1) * dh)
            qh = q_ref[:, hs].astype(_bf16)
            s_p = jnp.where(vis_p, _dot_nt(qh, kp) * scale - slope * dist_p, NEG)
            s_c = jnp.where(vis_c, _dot_nt(qh, kc) * scale - slope * dist_c, NEG)
            sink = sink_ref[h]
            m = jnp.maximum(jnp.maximum(jnp.max(s_p, axis=-1, keepdims=True),
                                        jnp.max(s_c, axis=-1, keepdims=True)), sink)
            e_p = jnp.exp(s_p - m)
            e_c = jnp.exp(s_c - m)
            den = (jnp.sum(e_p, axis=-1, keepdims=True) + jnp.sum(e_c, axis=-1, keepdims=True)
                   + jnp.exp(sink - m))
            inv = 1.0 / den
            o_ref[:, hs] = _dot((e_p * inv).astype(_bf16), vp) + _dot((e_c * inv).astype(_bf16), vc)


def _swa(qkv, sinks, cache_k, cache_v, *, t_prompt, n_batch_s, t_s, n_kv):
    n, width = qkv.shape
    dh = SWA_HEAD_DIM
    kvw = n_kv * dh
    qw = width - 2 * kvw
    group = qw // kvw
    win = WINDOW_CHUNKS * CHUNK
    kcol, vcol = qw // kvw, qw // kvw + 1
    common = dict(out_shape=jax.ShapeDtypeStruct((n, qw), _f32), compiler_params=_params("arbitrary", "arbitrary"))
    smem = pl.BlockSpec(memory_space=pltpu.SMEM)
    tq = win
    prev = lambda c: (lambda b, i: (jnp.maximum(i - 1, 0), c))
    o = pl.pallas_call(
        functools.partial(_swa_kernel, tq=tq, win=win, pos0=0, n_kv=n_kv, group=group),
        grid=(1, t_prompt // tq),
        in_specs=[smem,
                  pl.BlockSpec((tq, qw), lambda b, i: (i, 0)),
                  pl.BlockSpec((tq, kvw), lambda b, i: (i, kcol)),
                  pl.BlockSpec((tq, kvw), lambda b, i: (i, vcol)),
                  pl.BlockSpec((win, kvw), prev(kcol)),
                  pl.BlockSpec((win, kvw), prev(vcol))],
        out_specs=pl.BlockSpec((tq, qw), lambda b, i: (i, 0)),
        name="swa_prompt", **common,
    )(sinks, qkv, qkv, qkv, qkv, qkv)
    ob = t_prompt // t_s
    inner = functools.partial(_swa_kernel, tq=t_s, win=win, pos0=PAST_LEN, n_kv=n_kv, group=group)
    return pl.pallas_call(
        lambda full_ref, *refs: inner(*refs),
        grid=(n_batch_s, 1),
        in_specs=[pl.BlockSpec(memory_space=pl.ANY), smem,
                  pl.BlockSpec((t_s, qw), lambda b, i: (ob + b, 0)),
                  pl.BlockSpec((t_s, kvw), lambda b, i: (ob + b, kcol)),
                  pl.BlockSpec((t_s, kvw), lambda b, i: (ob + b, vcol)),
                  pl.BlockSpec((None, win, kvw), lambda b, i: (b, 0, 0)),
                  pl.BlockSpec((None, win, kvw), lambda b, i: (b, 0, 0))],
        out_specs=pl.BlockSpec((t_s, qw), lambda b, i: (ob + b, 0)),
        input_output_aliases={0: 0},
        name="swa_sample", **common,
    )(o, sinks, qkv, qkv, qkv, cache_k, cache_v)


def _mem_kernel(q_ref, k_ref, v_ref, o_ref, *, scale):
    s = _dot_nt(q_ref[...].astype(_bf16), k_ref[...].astype(_bf16)) * scale
    e = jnp.exp(s - jnp.max(s, axis=-1, keepdims=True))
    w = e * (1.0 / jnp.sum(e, axis=-1, keepdims=True))
    o_ref[...] = _dot(w.astype(_bf16), v_ref[...].astype(_bf16))


def _mem_attend(q, kv_p, k_s, v_s, *, t_prompt, n_batch_s, t_s, n_heads):
    n, d = q.shape
    dh = d // n_heads
    m = kv_p.shape[0]
    body = functools.partial(_mem_kernel, scale=dh ** -0.5)
    common = dict(out_shape=jax.ShapeDtypeStruct((n, d), _f32), compiler_params=_params("arbitrary", "arbitrary"))
    tq = _row_tile(t_prompt)
    o = pl.pallas_call(
        body,
        grid=(n_heads, t_prompt // tq),
        in_specs=[pl.BlockSpec((tq, dh), lambda h, i: (i, h)),
                  pl.BlockSpec((m, dh), lambda h, i: (0, h)),
                  pl.BlockSpec((m, dh), lambda h, i: (0, n_heads + h))],
        out_specs=pl.BlockSpec((tq, dh), lambda h, i: (i, h)),
        name="mem_prompt", **common,
    )(q, kv_p, kv_p)
    ob = t_prompt // t_s
    return pl.pallas_call(
        lambda full_ref, *refs: body(*refs),
        grid=(n_batch_s, n_heads),
        in_specs=[pl.BlockSpec(memory_space=pl.ANY),
                  pl.BlockSpec((t_s, dh), lambda b, h: (ob + b, h)),
                  pl.BlockSpec((None, m, dh), lambda b, h: (b, 0, h)),
                  pl.BlockSpec((None, m, dh), lambda b, h: (b, 0, h))],
        out_specs=pl.BlockSpec((t_s, dh), lambda b, h: (ob + b, h)),
        input_output_aliases={0: 0},
        name="mem_sample", **common,
    )(o, q, k_s, v_s)


def _router_kernel(x_ref, w_ref, b_ref, gate_ref, idx_ref):
    logits = jnp.dot(x_ref[...], w_ref[...], precision=lax.Precision.HIGHEST,
                     preferred_element_type=_f32) + b_ref[...]
    lane = lax.broadcasted_iota(jnp.int32, logits.shape, 1)
    lane_f = lane.astype(_f32)
    vals, ids = [], []
    for _ in range(TOP_K):
        m = jnp.max(logits, axis=-1, keepdims=True)
        am = jnp.min(jnp.where(logits == m, lane_f, float(LANE)), axis=-1, keepdims=True)
        vals.append(m)
        ids.append(am)
        logits = jnp.where(lane_f == am, -jnp.inf, logits)
    es = [jnp.exp(v - vals[0]) for v in vals]
    inv = 1.0 / sum(es)
    gates = jnp.zeros(logits.shape, _f32)
    idx = jnp.zeros(logits.shape, _f32)
    for k in range(TOP_K):
        gates = jnp.where(lane == k, es[k] * inv, gates)
        idx = jnp.where(lane == k, ids[k], idx)
    gate_ref[...] = gates
    idx_ref[...] = idx.astype(jnp.int32)


def _router(x, w_pad, b_pad):
    n, d = x.shape
    tm = _row_tile(n)
    return pl.pallas_call(
        _router_kernel,
        grid=(n // tm,),
        in_specs=[pl.BlockSpec((tm, d), lambda i: (i, 0)),
                  pl.BlockSpec((d, LANE), lambda i: (0, 0)),
                  pl.BlockSpec((1, LANE), lambda i: (0, 0))],
        out_specs=[pl.BlockSpec((tm, LANE), lambda i: (i, 0)), pl.BlockSpec((tm, LANE), lambda i: (i, 0))],
        out_shape=[jax.ShapeDtypeStruct((n, LANE), _f32), jax.ShapeDtypeStruct((n, LANE), jnp.int32)],
        compiler_params=_params("arbitrary"),
        name="router",
    )(x, w_pad, b_pad)


def _row_copy(src, s, dst, d, sem):
    return pltpu.make_async_copy(src.at[pl.ds(s, 1), :], dst.at[pl.ds(d, 1), :], sem)


def _scatter_kernel(dest_ref, x_ref, xs_in_ref, xs_ref, sem, *, tm):
    del xs_in_ref
    base = pl.program_id(0) * tm * TOP_K

    def issue(r, carry):
        for k in range(TOP_K):
            _row_copy(x_ref, r, xs_ref, dest_ref[base + r * TOP_K + k], sem).start()
        return carry

    def drain(r, carry):
        for k in range(TOP_K):
            _row_copy(x_ref, r, xs_ref, dest_ref[base + r * TOP_K + k], sem).wait()
        return carry

    lax.fori_loop(0, tm, issue, 0)
    lax.fori_loop(0, tm, drain, 0)


def _scatter_rows(x, dest, xs_buf):
    n, d = x.shape
    tm = _row_tile(n, 128)
    return pl.pallas_call(
        functools.partial(_scatter_kernel, tm=tm),
        grid_spec=pltpu.PrefetchScalarGridSpec(
            num_scalar_prefetch=1,
            grid=(n // tm,),
            in_specs=[pl.BlockSpec((tm, d), lambda i, dest: (i, 0)),
                      pl.BlockSpec(memory_space=pl.ANY)],
            out_specs=pl.BlockSpec(memory_space=pl.ANY),
            scratch_shapes=[pltpu.SemaphoreType.DMA(())]),
        out_shape=jax.ShapeDtypeStruct(xs_buf.shape, _f32),
        input_output_aliases={2: 0},
        compiler_params=_params("arbitrary"),
        name="moe_scatter",
    )(dest, x, xs_buf)


def _expert_kernel(sb_e_ref, sb_row_ref, sb_nblk_ref, sb_f_ref, xs_ref, wg_ref, wl_ref, bg_ref, bl_ref,
                   wd_ref, bd_ref, out_ref, xb_ref, acc_ref, wgb_ref, wlb_ref, wdb_ref, sem_in, sem_out, *, nf):
    del sb_e_ref, sb_f_ref
    s = pl.program_id(0)
    f = pl.program_id(1)
    nblk = sb_nblk_ref[s]
    row0 = pl.multiple_of(sb_row_ref[s], MOE_BLK)
    max_blk = xb_ref.shape[0] // MOE_BLK

    def rows_of(j):
        return pl.ds(j * MOE_BLK, MOE_BLK)

    def in_copy(j):
        return pltpu.make_async_copy(xs_ref.at[pl.ds(row0 + j * MOE_BLK, MOE_BLK), :],
                                     acc_ref.at[rows_of(j), :], sem_in.at[j])

    def out_copy(j):
        return pltpu.make_async_copy(acc_ref.at[rows_of(j), :],
                                     out_ref.at[pl.ds(row0 + j * MOE_BLK, MOE_BLK), :], sem_out.at[j])

    @pl.when((f == 0) & (nblk > 0))
    def _():
        for j in range(max_blk):
            @pl.when(j < nblk)
            def _():
                in_copy(j).start()
        for j in range(max_blk):
            @pl.when(j < nblk)
            def _():
                in_copy(j).wait()
                xb_ref[rows_of(j), :] = acc_ref[rows_of(j), :].astype(_bf16)
                acc_ref[rows_of(j), :] = jnp.zeros((MOE_BLK, acc_ref.shape[1]), _f32)

    @pl.when(nblk > 0)
    def _():
        wgb_ref[...] = wg_ref[...].astype(_bf16)
        wlb_ref[...] = wl_ref[...].astype(_bf16)
        wdb_ref[...] = wd_ref[...].astype(_bf16)

        def block(j, carry):
            rows = pl.ds(pl.multiple_of(j * MOE_BLK, MOE_BLK), MOE_BLK)
            x = xb_ref[rows, :]
            gate = jnp.minimum(_dot(x, wgb_ref[...]) + bg_ref[...], SWIGLU_LIMIT)
            lin = jnp.clip(_dot(x, wlb_ref[...]) + bl_ref[...], -SWIGLU_LIMIT, SWIGLU_LIMIT)
            act = gate * jax.nn.sigmoid(SWIGLU_ALPHA * gate) * (lin + 1.0)
            acc_ref[rows, :] += _dot(act.astype(_bf16), wdb_ref[...])
            return carry

        lax.fori_loop(0, nblk, block, 0)

    @pl.when((f == nf - 1) & (nblk > 0))
    def _():
        for j in range(max_blk):
            @pl.when(j < nblk)
            def _():
                acc_ref[rows_of(j), :] += bd_ref[...]
                out_copy(j).start()
        for j in range(max_blk):
            @pl.when(j < nblk)
            def _():
                out_copy(j).wait()


def _experts(xs, tables, w_gu, b_gu, w_d, b_d, layer):
    p, d = xs.shape
    n_e, _, ff2 = w_gu.shape[1:]
    ff = ff2 // 2
    ft = min(MOE_FT, ff)
    nf = ff // ft
    n_sb = tables[0].shape[0]
    b_gu3 = b_gu.reshape(b_gu.shape[0], n_e, 1, ff2)
    b_d3 = b_d.reshape(b_d.shape[0], n_e, 1, d)
    return pl.pallas_call(
        functools.partial(_expert_kernel, nf=nf),
        grid_spec=pltpu.PrefetchScalarGridSpec(
            num_scalar_prefetch=4,
            grid=(n_sb, nf),
            in_specs=[pl.BlockSpec(memory_space=pl.ANY),
                      pl.BlockSpec((None, None, d, ft), lambda s, f, e, r, nb, fs: (layer, e[s], 0, fs[s * nf + f])),
                      pl.BlockSpec((None, None, d, ft), lambda s, f, e, r, nb, fs: (layer, e[s], 0, nf + fs[s * nf + f])),
                      pl.BlockSpec((None, None, 1, ft), lambda s, f, e, r, nb, fs: (layer, e[s], 0, fs[s * nf + f])),
                      pl.BlockSpec((None, None, 1, ft), lambda s, f, e, r, nb, fs: (layer, e[s], 0, nf + fs[s * nf + f])),
                      pl.BlockSpec((None, None, ft, d), lambda s, f, e, r, nb, fs: (layer, e[s], fs[s * nf + f], 0)),
                      pl.BlockSpec((None, None, 1, d), lambda s, f, e, r, nb, fs: (layer, e[s], 0, 0))],
            out_specs=pl.BlockSpec(memory_space=pl.ANY),
            scratch_shapes=[pltpu.VMEM((MOE_SUPER, d), _bf16), pltpu.VMEM((MOE_SUPER, d), _f32),
                            pltpu.VMEM((d, ft), _bf16), pltpu.VMEM((d, ft), _bf16), pltpu.VMEM((ft, d), _bf16),
                            pltpu.SemaphoreType.DMA((MOE_SUPER // MOE_BLK,)),
                            pltpu.SemaphoreType.DMA((MOE_SUPER // MOE_BLK,))]),
        out_shape=jax.ShapeDtypeStruct((p, d), _f32),
        compiler_params=_params("arbitrary", "arbitrary"),
        name="moe_experts",
    )(*tables, xs, w_gu, w_gu, b_gu3, b_gu3, w_d, b_d3)


def _combine_kernel(dest_ref, y_ref, gate_ref, res_ref, g_ref, be_ref, o_ref, buf_ref, sem, *, tm, alpha):
    base = pl.program_id(0) * tm * TOP_K

    def issue(r, carry):
        for k in range(TOP_K):
            _row_copy(y_ref, dest_ref[base + r * TOP_K + k], buf_ref.at[k], r, sem).start()
        return carry

    def drain(r, carry):
        for k in range(TOP_K):
            _row_copy(y_ref, dest_ref[base + r * TOP_K + k], buf_ref.at[k], r, sem).wait()
        return carry

    lax.fori_loop(0, tm, issue, 0)
    lax.fori_loop(0, tm, drain, 0)
    gates = gate_ref[...]
    mix = gates[:, 0:1] * buf_ref[0]
    for k in range(1, TOP_K):
        mix = mix + gates[:, k:k + 1] * buf_ref[k]
    o_ref[...] = _ln(alpha * res_ref[...] + mix, g_ref[...], be_ref[...])


def _combine(y, dest, gates, res, g, be, *, alpha):
    n, d = res.shape
    tm = _row_tile(n, 128)
    return pl.pallas_call(
        functools.partial(_combine_kernel, tm=tm, alpha=alpha),
        grid_spec=pltpu.PrefetchScalarGridSpec(
            num_scalar_prefetch=1,
            grid=(n // tm,),
            in_specs=[pl.BlockSpec(memory_space=pl.ANY),
                      pl.BlockSpec((tm, LANE), lambda i, dest: (i, 0)),
                      pl.BlockSpec((tm, d), lambda i, dest: (i, 0)),
                      pl.BlockSpec((1, d), lambda i, dest: (0, 0)),
                      pl.BlockSpec((1, d), lambda i, dest: (0, 0))],
            out_specs=pl.BlockSpec((tm, d), lambda i, dest: (i, 0)),
            scratch_shapes=[pltpu.VMEM((TOP_K, tm, d), _f32), pltpu.SemaphoreType.DMA(())]),
        out_shape=jax.ShapeDtypeStruct((n, d), _f32),
        compiler_params=_params("arbitrary"),
        name="moe_combine",
    )(dest, y, gates, res, g, be)


def _moe_tables(idx, n_e, n_sb, nf):
    flat_e = idx[:, :TOP_K].reshape(-1)
    onehot = (flat_e[:, None] == jnp.arange(n_e, dtype=jnp.int32)[None, :]).astype(jnp.int32)
    cum = jnp.cumsum(onehot, axis=0)
    counts = cum[-1]
    rank = jnp.sum(onehot * cum, axis=1) - 1
    padded = (counts + MOE_BLK - 1) // MOE_BLK * MOE_BLK
    pad_start = jnp.cumsum(padded) - padded
    dest = (pad_start[flat_e] + rank).astype(jnp.int32)

    n_super = (padded + MOE_SUPER - 1) // MOE_SUPER
    sb_end = jnp.cumsum(n_super)
    total = sb_end[-1]
    s = jnp.arange(n_sb, dtype=jnp.int32)
    valid = s < total
    s_c = jnp.minimum(s, total - 1)
    e = jnp.minimum(jnp.searchsorted(sb_end, s_c, side='right'), n_e - 1).astype(jnp.int32)
    j = s_c - (sb_end[e] - n_super[e])
    row0 = (pad_start[e] + j * MOE_SUPER).astype(jnp.int32)
    nblk = jnp.clip((padded[e] - j * MOE_SUPER) // MOE_BLK, 0, MOE_SUPER // MOE_BLK)
    nblk = jnp.where(valid, nblk, 0).astype(jnp.int32)
    f_idx = jnp.where(valid[:, None], jnp.arange(nf, dtype=jnp.int32)[None, :], nf - 1).reshape(-1).astype(jnp.int32)
    return dest, (e, row0, nblk, f_idx)


def kernel(x_prompt, x_sample, mem_prompt, state_conv, state_hgrn, cache_swa_k, cache_swa_v, cache_mem_k, cache_mem_v, ln_g, ln_b, conv_w_in, conv_b_in, conv_w_dw, conv_b_dw, conv_ln_g, conv_ln_b, conv_w_out, conv_b_out, hgrn_w_in, hgrn_lb, hgrn_norm_g, hgrn_w_out, swa_w_qkv, swa_b_qkv, swa_sinks, swa_w_out, swa_b_out, mem_w_q, mem_w_kv, mem_w_out, moe_w_router, moe_b_router, moe_w_gate_up, moe_b_gate_up, moe_w_down, moe_b_down):
    bp, tp, d = x_prompt.shape
    bs, ts, _ = x_sample.shape
    assert bp == 1, "the prompt group is one stream"
    depth = ln_g.shape[0]
    n = tp + bs * ts
    n_e = moe_w_router.shape[-1]
    n_kv = cache_swa_k.shape[3]
    keep = cache_swa_k.shape[2]
    mem_heads = cache_mem_k.shape[3]
    m_tok = mem_prompt.shape[1]
    kw = conv_w_dw.shape[1]
    conv_state = kw - 1
    nh = d // HGRN_HEAD_DIM
    alpha = (2 * depth) ** 0.25
    grp = dict(t_prompt=tp, n_batch_s=bs, t_s=ts)
    assert ts >= conv_state and keep == WINDOW_CHUNKS * CHUNK and n_e <= LANE

    def row(v):
        return v.reshape(1, -1).astype(_f32)

    zeros_d = jnp.zeros((1, d), _f32)
    lb_p = jax.nn.softmax(hgrn_lb.astype(_f32), axis=0)
    lower_bounds = jnp.cumsum(lb_p, axis=0) - lb_p[0]

    n_assign = n * TOP_K
    p_rows = ((n_assign + n_e * (MOE_BLK - 1)) // MOE_BLK + 1) * MOE_BLK
    n_sb = (n_assign + n_e * (MOE_BLK - 1)) // MOE_SUPER + n_e
    nf = moe_w_down.shape[2] // min(MOE_FT, moe_w_down.shape[2])
    xs_buf = jnp.zeros((p_rows, d), _f32)

    x = jnp.concatenate([x_prompt.reshape(tp, d), x_sample.reshape(bs * ts, d)], axis=0)
    p_conv, s_conv, p_hgrn, s_hgrn, p_k, p_v, s_k, s_v, p_mk, p_mv = ([] for _ in range(10))

    for i in range(depth):
        kind, slot = i % 3, i // 3
        if kind == 0:
            u = _mm_glu(x, conv_w_in[slot].astype(_bf16), row(conv_b_in[slot]))
            hist = jnp.pad(state_conv[slot], ((0, 0), (HIST_ROWS - conv_state, 0), (0, 0)))
            c = _conv(u, hist, conv_w_dw[slot], row(conv_b_dw[slot]), **grp)
            p_conv.append(u[tp - conv_state:tp][None])
            s_conv.append(u[tp:].reshape(bs, ts, d)[:, ts - conv_state:])
            x = _mm_res_ln(c, conv_w_out[slot].astype(_bf16), row(conv_b_out[slot]), x,
                           row(ln_g[i, 0]), row(ln_b[i, 0]), alpha=alpha,
                           pre_norm=(row(conv_ln_g[slot]), row(conv_ln_b[slot])))
        elif kind == 1:
            proj = _mm(x, hgrn_w_in[slot].astype(_bf16), jnp.zeros((1, 4 * d), _f32))
            lbv, ngv = row(lower_bounds[i]), row(hgrn_norm_g[slot])
            s0_p = jnp.zeros((1, nh, HGRN_HEAD_DIM, HGRN_HEAD_DIM), _f32)
            s0_s = jnp.swapaxes(state_hgrn[slot].astype(_f32), -1, -2)
            o, sf_p = _hgrn(proj, lbv, ngv, s0_p, None, row_off=0, n_batch=1, t_len=tp, name="hgrn_prompt")
            o, sf_s = _hgrn(proj, lbv, ngv, s0_s, o, row_off=tp, n_batch=bs, t_len=ts, name="hgrn_sample")
            p_hgrn.append(jnp.swapaxes(sf_p, -1, -2))
            s_hgrn.append(jnp.swapaxes(sf_s, -1, -2))
            x = _mm_res_ln(o, hgrn_w_out[slot].astype(_bf16), zeros_d, x,
                           row(ln_g[i, 0]), row(ln_b[i, 0]), alpha=alpha)
        else:
            qkv = _mm(x, swa_w_qkv[slot].astype(_bf16), row(swa_b_qkv[slot]), tn=swa_w_qkv.shape[-1])
            kvw = n_kv * SWA_HEAD_DIM
            qw = qkv.shape[1] - 2 * kvw
            ck = cache_swa_k[slot].reshape(bs, keep, kvw)
            cv = cache_swa_v[slot].reshape(bs, keep, kvw)
            o = _swa(qkv, swa_sinks[slot].astype(_f32), ck, cv, n_kv=n_kv, **grp)
            k_new, v_new = qkv[:, qw:qw + kvw], qkv[:, qw + kvw:]
            p_k.append(k_new[tp - keep:tp].reshape(1, keep, n_kv, SWA_HEAD_DIM))
            p_v.append(v_new[tp - keep:tp].reshape(1, keep, n_kv, SWA_HEAD_DIM))
            s_k.append(jnp.concatenate([ck, k_new[tp:].reshape(bs, ts, kvw)], axis=1)[:, -keep:]
                       .reshape(bs, keep, n_kv, SWA_HEAD_DIM))
            s_v.append(jnp.concatenate([cv, v_new[tp:].reshape(bs, ts, kvw)], axis=1)[:, -keep:]
                       .reshape(bs, keep, n_kv, SWA_HEAD_DIM))
            x = _mm_res_ln(o, swa_w_out[slot].astype(_bf16), row(swa_b_out[slot]), x,
                           row(ln_g[i, 0]), row(ln_b[i, 0]), alpha=alpha)

        kv_p = _mm(mem_prompt.reshape(m_tok, d), mem_w_kv[i].astype(_bf16), jnp.zeros((1, 2 * d), _f32))
        p_mk.append(kv_p[:, :d].reshape(1, m_tok, mem_heads, d // mem_heads))
        p_mv.append(kv_p[:, d:].reshape(1, m_tok, mem_heads, d // mem_heads))
        q = _mm(x, mem_w_q[i].astype(_bf16), zeros_d)
        o = _mem_attend(q, kv_p, cache_mem_k[i].reshape(bs, m_tok, d), cache_mem_v[i].reshape(bs, m_tok, d),
                        n_heads=mem_heads, **grp)
        x = _mm_res_ln(o, mem_w_out[i].astype(_bf16), zeros_d, x, row(ln_g[i, 1]), row(ln_b[i, 1]), alpha=alpha)

        w_r = jnp.pad(moe_w_router[i].astype(_f32), ((0, 0), (0, LANE - n_e)))
        b_r = jnp.pad(moe_b_router[i].astype(_f32), (0, LANE - n_e), constant_values=NEG).reshape(1, LANE)
        gates, idx = _router(x, w_r, b_r)
        dest, tables = _moe_tables(idx, n_e, n_sb, nf)
        xs_buf = _scatter_rows(x, dest, xs_buf)
        y = _experts(xs_buf, tables, moe_w_gate_up, moe_b_gate_up, moe_w_down, moe_b_down, i)
        x = _combine(y, dest, gates, x, row(ln_g[i, 2]), row(ln_b[i, 2]), alpha=alpha)

    y_prompt = x[:tp].reshape(1, tp, d)
    y_sample = x[tp:].reshape(bs, ts, d)
    return (y_prompt, y_sample, jnp.stack(p_conv), jnp.stack(p_hgrn), jnp.stack(p_k), jnp.stack(p_v),
            jnp.stack(p_mk), jnp.stack(p_mv), jnp.stack(s_conv), jnp.stack(s_hgrn), jnp.stack(s_k), jnp.stack(s_v))
```

```python
import functools

import jax
import jax.numpy as jnp
from jax import lax
from jax.experimental import pallas as pl
from jax.experimental.pallas import tpu as pltpu

PAST_LEN = 1024
CHUNK = 64
WINDOW_CHUNKS = 2
SWA_HEAD_DIM = 64
HGRN_HEAD_DIM = 128
TOP_K = 4
SWIGLU_LIMIT = 7.0
SWIGLU_ALPHA = 1.702
LN_EPS = 1e-5

LANE = 128
SUBLANE = 8
VMEM_LIMIT = 56 * 1024 * 1024
HIST_ROWS = 32
MOE_BLK = 256
MOE_SUPER = 2048
MOE_STAGE = 2
MOE_FT = 256
HGRN_SUB = 16
HGRN_HEADS_PER_STEP = 4
NEG = -1e30

_bf16 = jnp.bfloat16
_f32 = jnp.float32


def _params(*sem):
    return pltpu.CompilerParams(dimension_semantics=sem, vmem_limit_bytes=VMEM_LIMIT)


def _row_tile(n, cap=256):
    for t in (cap, 128, 64, 32, 16, 8):
        if t <= cap and n % t == 0:
            return t
    raise ValueError(f"row count {n} is not a multiple of {SUBLANE}")


def _tail_steps(n, n_own, tile):
    tail = n - n_own
    assert tail % tile == 0, "rows after a call's own must fill whole tiles"
    return tail // tile


def _ln(y, g, b):
    mu = jnp.mean(y, axis=-1, keepdims=True)
    d = y - mu
    var = jnp.mean(d * d, axis=-1, keepdims=True)
    return d * lax.rsqrt(var + LN_EPS) * g + b


def _dot(a, b):
    return jnp.dot(a, b, preferred_element_type=_f32)


def _dot_nt(a, b):
    return lax.dot_general(a, b, (((1,), (1,)), ((), ())), preferred_element_type=_f32)


def _mm_kernel(x_ref, w_ref, b_ref, o_ref):
    o_ref[...] = _dot(x_ref[...].astype(_bf16), w_ref[...]) + b_ref[...]


def _mm(x, w, b, *, tn=2048):
    m, k = x.shape
    n = w.shape[1]
    tn = min(tn, n)
    tm = _row_tile(m)
    return pl.pallas_call(
        _mm_kernel,
        grid=(n // tn, m // tm),
        in_specs=[pl.BlockSpec((tm, k), lambda j, i: (i, 0)),
                  pl.BlockSpec((k, tn), lambda j, i: (0, j)),
                  pl.BlockSpec((1, tn), lambda j, i: (0, j))],
        out_specs=pl.BlockSpec((tm, tn), lambda j, i: (i, j)),
        out_shape=jax.ShapeDtypeStruct((m, n), _f32),
        compiler_params=_params("arbitrary", "arbitrary"),
        name="mm_bias",
    )(x, w, b)


def _mm_glu_kernel(x_ref, wa_ref, wb_ref, ba_ref, bb_ref, o_ref):
    x = x_ref[...].astype(_bf16)
    a = _dot(x, wa_ref[...]) + ba_ref[...]
    g = _dot(x, wb_ref[...]) + bb_ref[...]
    o_ref[...] = a * jax.nn.sigmoid(g)


def _mm_glu(x, w, b, *, tn=1024):
    m, k = x.shape
    n = w.shape[1] // 2
    tn = min(tn, n)
    tm = _row_tile(m)
    nj = n // tn
    return pl.pallas_call(
        _mm_glu_kernel,
        grid=(nj, m // tm),
        in_specs=[pl.BlockSpec((tm, k), lambda j, i: (i, 0)),
                  pl.BlockSpec((k, tn), lambda j, i: (0, j)),
                  pl.BlockSpec((k, tn), lambda j, i: (0, j + nj)),
                  pl.BlockSpec((1, tn), lambda j, i: (0, j)),
                  pl.BlockSpec((1, tn), lambda j, i: (0, j + nj))],
        out_specs=pl.BlockSpec((tm, tn), lambda j, i: (i, j)),
        out_shape=jax.ShapeDtypeStruct((m, n), _f32),
        compiler_params=_params("arbitrary", "arbitrary"),
        name="mm_glu",
    )(x, w, w, b, b)


def _mm_res_ln_kernel(x_ref, w_ref, b_ref, res_ref, g_ref, be_ref, o_ref, *, alpha):
    h = _dot(x_ref[...].astype(_bf16), w_ref[...]) + b_ref[...]
    o_ref[...] = _ln(alpha * res_ref[...] + h, g_ref[...], be_ref[...])


def _mm_norm_res_ln_kernel(x_ref, ng_ref, nb_ref, w_ref, b_ref, res_ref, g_ref, be_ref, o_ref, *, alpha):
    c = _ln(x_ref[...], ng_ref[...], nb_ref[...])
    c = c * jax.nn.sigmoid(c)
    h = _dot(c.astype(_bf16), w_ref[...]) + b_ref[...]
    o_ref[...] = _ln(alpha * res_ref[...] + h, g_ref[...], be_ref[...])


def _mm_res_ln(x, w, b, res, g, be, *, alpha, pre_norm=None):
    m, k = x.shape
    n = w.shape[1]
    tm = _row_tile(m)
    row = lambda i: (i, 0)
    fix = lambda i: (0, 0)
    specs = [pl.BlockSpec((tm, k), row)]
    args = [x]
    if pre_norm is not None:
        specs += [pl.BlockSpec((1, k), fix), pl.BlockSpec((1, k), fix)]
        args += list(pre_norm)
        body = _mm_norm_res_ln_kernel
    else:
        body = _mm_res_ln_kernel
    specs += [pl.BlockSpec((k, n), fix), pl.BlockSpec((1, n), fix), pl.BlockSpec((tm, n), row),
              pl.BlockSpec((1, n), fix), pl.BlockSpec((1, n), fix)]
    args += [w, b, res, g, be]
    return pl.pallas_call(
        functools.partial(body, alpha=alpha),
        grid=(m // tm,),
        in_specs=specs,
        out_specs=pl.BlockSpec((tm, n), row),
        out_shape=jax.ShapeDtypeStruct((m, n), _f32),
        compiler_params=_params("arbitrary"),
        name="mm_res_ln",
    )(*args)


def _conv_kernel(hist_ref, u_ref, w_ref, b_ref, o_ref, buf_ref, *, n_real, tt, lb, kw):
    i = pl.program_id(0)

    def compute():
        hist = hist_ref[...]
        if n_real is not None:
            hist = jnp.where(i == 0, 0.0, hist)
        buf_ref[0:HIST_ROWS, :] = hist
        buf_ref[HIST_ROWS:, :] = u_ref[...]
        off = HIST_ROWS - (kw - 1)
        rb = min(tt, 64)
        for r in range(tt // rb):
            for c in range(lb // LANE):
                cs = slice(c * LANE, (c + 1) * LANE)
                acc = jnp.zeros((rb, LANE), _f32)
                for j in range(kw):
                    acc = acc + w_ref[j:j + 1, cs] * buf_ref[r * rb + off + j:r * rb + off + j + rb, cs]
                o_ref[r * rb:(r + 1) * rb, cs] = acc + b_ref[:, cs]

    if n_real is None:
        compute()
    else:
        pl.when(i < n_real)(compute)

        @pl.when(i >= n_real)
        def _():
            o_ref[...] = jnp.zeros(o_ref.shape, _f32)


def _conv(u, hist_s, w, b, *, t_prompt, n_batch_s, t_s):
    n, d = u.shape
    kw = w.shape[0]
    lb = min(512, d)
    tt = _row_tile(t_prompt)
    hb = tt // HIST_ROWS
    nt = t_prompt // tt
    last = nt - 1
    common = dict(out_shape=jax.ShapeDtypeStruct((n, d), _f32), compiler_params=_params("arbitrary", "arbitrary"))
    c = pl.pallas_call(
        functools.partial(_conv_kernel, n_real=nt, tt=tt, lb=lb, kw=kw),
        grid=(nt + _tail_steps(n, t_prompt, tt), d // lb),
        in_specs=[pl.BlockSpec((HIST_ROWS, lb), lambda i, j: (jnp.maximum(jnp.minimum(i, last) * hb - 1, 0), j)),
                  pl.BlockSpec((tt, lb), lambda i, j: (jnp.minimum(i, last), j)),
                  pl.BlockSpec((kw, lb), lambda i, j: (0, j)),
                  pl.BlockSpec((1, lb), lambda i, j: (0, j))],
        out_specs=pl.BlockSpec((tt, lb), lambda i, j: (i, j)),
        scratch_shapes=[pltpu.VMEM((HIST_ROWS + tt, lb), _f32)],
        name="conv_prompt", **common,
    )(u, u, w, b)
    ob = t_prompt // t_s
    return pl.pallas_call(
        lambda full_ref, *refs: _conv_kernel(*refs, n_real=None, tt=t_s, lb=lb, kw=kw),
        grid=(n_batch_s, d // lb),
        in_specs=[pl.BlockSpec(memory_space=pl.ANY),
                  pl.BlockSpec((None, HIST_ROWS, lb), lambda i, j: (i, 0, j)),
                  pl.BlockSpec((t_s, lb), lambda i, j: (ob + i, j)),
                  pl.BlockSpec((kw, lb), lambda i, j: (0, j)),
                  pl.BlockSpec((1, lb), lambda i, j: (0, j))],
        out_specs=pl.BlockSpec((t_s, lb), lambda i, j: (ob + i, j)),
        scratch_shapes=[pltpu.VMEM((HIST_ROWS + t_s, lb), _f32)],
        input_output_aliases={0: 0},
        name="conv_sample", **common,
    )(c, hist_s, u, w, b)


def _hgrn_kernel(q_ref, f_ref, i_ref, og_ref, lb_ref, ng_ref, s0_ref, o_ref, sf_ref, st_ref, *,
                 tt, blk, heads, n_real):
    t = pl.program_id(2)
    hd = HGRN_HEAD_DIM

    @pl.when(t == 0)
    def _():
        st_ref[...] = s0_ref[...]

    rr = lax.broadcasted_iota(jnp.int32, (blk, blk), 0)
    cc = lax.broadcasted_iota(jnp.int32, (blk, blk), 1)
    tril = (cc <= rr).astype(_f32)
    sub_row = lax.broadcasted_iota(jnp.int32, (HGRN_SUB, LANE), 0)

    def head_chunk(base, hh):
        rows = pl.ds(base, blk)
        ls = slice(hh * hd, (hh + 1) * hd)
        lbv = lb_ref[:, ls]
        q = q_ref[rows, ls]
        qh = q * jax.nn.sigmoid(q)
        forget = lbv + (1.0 - lbv) * jax.nn.sigmoid(f_ref[rows, ls])
        kh = 1.0 - forget
        v = i_ref[rows, ls]
        b = jnp.dot(tril, jnp.log(forget), precision=lax.Precision.HIGHEST, preferred_element_type=_f32)
        st = st_ref[hh]
        inter = _dot_nt((qh * jnp.exp(b)).astype(_bf16), st.astype(_bf16))
        b_last = b[blk - 1:blk, :]
        kf_end = (kh * jnp.exp(b_last - b)).astype(_bf16)
        st_ref[hh] = st * jnp.exp(b_last) + lax.dot_general(
            v.astype(_bf16), kf_end, (((0,), (0,)), ((), ())), preferred_element_type=_f32)
        vb = v.astype(_bf16)
        for blk_i in range(blk // HGRN_SUB):
            r0 = blk_i * HGRN_SUB
            b_i = b[r0:r0 + HGRN_SUB, :]
            q_i = qh[r0:r0 + HGRN_SUB, :]
            acc = inter[r0:r0 + HGRN_SUB, :]
            for s in range(HGRN_SUB):
                keep = sub_row >= s
                decay = jnp.exp(jnp.where(keep, b_i - b[r0 + s:r0 + s + 1, :], NEG))
                col = jnp.sum(q_i * decay * kh[r0 + s:r0 + s + 1, :], axis=-1, keepdims=True)
                acc = acc + col * v[r0 + s:r0 + s + 1, :]
            if r0 > 0:
                anchor = b[r0 - 1:r0, :]
                qf = (q_i * jnp.exp(b_i - anchor)).astype(_bf16)
                kf = (kh[0:r0, :] * jnp.exp(anchor - b[0:r0, :])).astype(_bf16)
                acc = acc + _dot(_dot_nt(qf, kf).astype(_bf16), vb[0:r0, :])
            o = acc * lax.rsqrt(jnp.mean(acc * acc, axis=-1, keepdims=True) + LN_EPS)
            o_rows = pl.ds(base + r0, HGRN_SUB)
            o_ref[o_rows, ls] = o * ng_ref[:, ls] * jax.nn.sigmoid(og_ref[o_rows, ls])

    def chunk(c, carry):
        base = pl.multiple_of(c * blk, blk)
        for hh in range(heads):
            head_chunk(base, hh)
        return carry

    @pl.when(t < n_real)
    def _():
        lax.fori_loop(0, tt // blk, chunk, 0)

    @pl.when(t >= n_real)
    def _():
        o_ref[...] = jnp.zeros(o_ref.shape, _f32)

    @pl.when(t == n_real - 1)
    def _():
        sf_ref[...] = st_ref[...]


def _hgrn(proj, lb, ng, s0_t, o_prev, *, row_off, n_batch, t_len, name):
    n, d4 = proj.shape
    d = d4 // 4
    hd = HGRN_HEAD_DIM
    nh = d // hd
    heads = HGRN_HEADS_PER_STEP if nh % HGRN_HEADS_PER_STEP == 0 else 1
    n_groups = nh // heads
    w = heads * hd
    blk = min(t_len, CHUNK)
    tt = _row_tile(t_len)
    nt = t_len // tt
    last = nt - 1
    ob = row_off // tt
    tail = _tail_steps(n, row_off + n_batch * t_len, tt) if o_prev is None else 0
    row = lambda k: (lambda b, h, t: (ob + b * nt + jnp.minimum(t, last), k * n_groups + h))
    head = lambda b, h, t: (0, h)
    in_specs = [pl.BlockSpec((tt, w), row(0)), pl.BlockSpec((tt, w), row(1)),
                pl.BlockSpec((tt, w), row(2)), pl.BlockSpec((tt, w), row(3)),
                pl.BlockSpec((1, w), head), pl.BlockSpec((1, w), head),
                pl.BlockSpec((None, heads, hd, hd), lambda b, h, t: (b, h, 0, 0))]
    args = [proj, proj, proj, proj, lb, ng, s0_t]
    body = functools.partial(_hgrn_kernel, tt=tt, blk=blk, heads=heads, n_real=nt)
    aliases = {}
    if o_prev is not None:
        in_specs = [pl.BlockSpec(memory_space=pl.ANY)] + in_specs
        args = [o_prev] + args
        inner = body
        body = lambda full_ref, *refs: inner(*refs)
        aliases = {0: 0}
    return pl.pallas_call(
        body,
        grid=(n_batch, n_groups, nt + tail),
        in_specs=in_specs,
        out_specs=[pl.BlockSpec((tt, w), lambda b, h, t: (ob + b * nt + t, h)),
                   pl.BlockSpec((None, heads, hd, hd), lambda b, h, t: (b, h, 0, 0))],
        out_shape=[jax.ShapeDtypeStruct((n, d), _f32),
                   jax.ShapeDtypeStruct((n_batch, nh, hd, hd), _f32)],
        scratch_shapes=[pltpu.VMEM((heads, hd, hd), _f32)],
        input_output_aliases=aliases,
        compiler_params=_params("arbitrary", "arbitrary", "arbitrary"),
        name=name,
    )(*args)


def _swa_kernel(sink_ref, q_ref, kc_ref, vc_ref, kp_ref, vp_ref, o_ref, *, tq, win, pos0, n_kv, group, n_real):
    dh = SWA_HEAD_DIM
    i = pl.program_id(1)
    n_heads = n_kv * group

    def compute():
        qpos0 = pos0 + i * tq

        def bias_terms(nk, kpos0):
            qp = qpos0 + lax.broadcasted_iota(jnp.int32, (tq, nk), 0)
            kp = kpos0 + lax.broadcasted_iota(jnp.int32, (tq, nk), 1)
            qc = qp // CHUNK
            kc = jnp.maximum(kp, 0) // CHUNK
            vis = (kp >= 0) & (kc <= qc) & (qc - kc <= WINDOW_CHUNKS)
            return vis, jnp.abs(qp - kp).astype(_f32)

        vis_p, dist_p = bias_terms(win, qpos0 - win)
        vis_c, dist_c = bias_terms(tq, qpos0)
        scale = dh ** -0.5
        for kv in range(n_kv):
            ks = slice(kv * dh, (kv + 1) * dh)
            kp = kp_ref[:, ks].astype(_bf16)
            kc = kc_ref[:, ks].astype(_bf16)
            vp = vp_ref[:, ks].astype(_bf16)
            vc = vc_ref[:, ks].astype(_bf16)
            heads = [kv * group + g for g in range(group)]
            qs = jnp.concatenate([q_ref[:, h * dh:(h + 1) * dh] for h in heads], axis=0).astype(_bf16)
            raw_p = _dot_nt(qs, kp)
            raw_c = _dot_nt(qs, kc)
            w_p, w_c = [], []
            for g, h in enumerate(heads):
                slope = 2.0 ** (-8.0 * (h + 1) / n_heads)
                rs = slice(g * tq, (g + 1) * tq)
                s_p = jnp.where(vis_p, raw_p[rs] * scale - slope * dist_p, NEG)
                s_c = jnp.where(vis_c, raw_c[rs] * scale - slope * dist_c, NEG)
                sink = sink_ref[h]
                m = jnp.maximum(jnp.maximum(jnp.max(s_p, axis=-1, keepdims=True),
                                            jnp.max(s_c, axis=-1, keepdims=True)), sink)
                e_p = jnp.exp(s_p - m)
                e_c = jnp.exp(s_c - m)
                den = (jnp.sum(e_p, axis=-1, keepdims=True) + jnp.sum(e_c, axis=-1, keepdims=True)
                       + jnp.exp(sink - m))
                inv = 1.0 / den
                w_p.append((e_p * inv).astype(_bf16))
                w_c.append((e_c * inv).astype(_bf16))
            o = _dot(jnp.concatenate(w_p, axis=0), vp) + _dot(jnp.concatenate(w_c, axis=0), vc)
            for g, h in enumerate(heads):
                o_ref[:, h * dh:(h + 1) * dh] = o[g * tq:(g + 1) * tq]

    if n_real is None:
        compute()
    else:
        pl.when(i < n_real)(compute)

        @pl.when(i >= n_real)
        def _():
            o_ref[...] = jnp.zeros(o_ref.shape, _f32)


def _swa(qkv, sinks, cache_k, cache_v, *, t_prompt, n_batch_s, t_s, n_kv):
    n, width = qkv.shape
    dh = SWA_HEAD_DIM
    kvw = n_kv * dh
    qw = width - 2 * kvw
    group = qw // kvw
    win = WINDOW_CHUNKS * CHUNK
    kcol, vcol = qw // kvw, qw // kvw + 1
    common = dict(out_shape=jax.ShapeDtypeStruct((n, qw), _f32), compiler_params=_params("arbitrary", "arbitrary"))
    smem = pl.BlockSpec(memory_space=pltpu.SMEM)
    tq = win
    nt = t_prompt // tq
    last = nt - 1
    cur = lambda c: (lambda b, i: (jnp.minimum(i, last), c))
    prev = lambda c: (lambda b, i: (jnp.maximum(jnp.minimum(i, last) - 1, 0), c))
    o = pl.pallas_call(
        functools.partial(_swa_kernel, tq=tq, win=win, pos0=0, n_kv=n_kv, group=group, n_real=nt),
        grid=(1, nt + _tail_steps(n, t_prompt, tq)),
        in_specs=[smem,
                  pl.BlockSpec((tq, qw), cur(0)),
                  pl.BlockSpec((tq, kvw), cur(kcol)),
                  pl.BlockSpec((tq, kvw), cur(vcol)),
                  pl.BlockSpec((win, kvw), prev(kcol)),
                  pl.BlockSpec((win, kvw), prev(vcol))],
        out_specs=pl.BlockSpec((tq, qw), lambda b, i: (i, 0)),
        name="swa_prompt", **common,
    )(sinks, qkv, qkv, qkv, qkv, qkv)
    ob = t_prompt // t_s
    inner = functools.partial(_swa_kernel, tq=t_s, win=win, pos0=PAST_LEN, n_kv=n_kv, group=group, n_real=None)
    return pl.pallas_call(
        lambda full_ref, *refs: inner(*refs),
        grid=(n_batch_s, 1),
        in_specs=[pl.BlockSpec(memory_space=pl.ANY), smem,
                  pl.BlockSpec((t_s, qw), lambda b, i: (ob + b, 0)),
                  pl.BlockSpec((t_s, kvw), lambda b, i: (ob + b, kcol)),
                  pl.BlockSpec((t_s, kvw), lambda b, i: (ob + b, vcol)),
                  pl.BlockSpec((None, win, kvw), lambda b, i: (b, 0, 0)),
                  pl.BlockSpec((None, win, kvw), lambda b, i: (b, 0, 0))],
        out_specs=pl.BlockSpec((t_s, qw), lambda b, i: (ob + b, 0)),
        input_output_aliases={0: 0},
        name="swa_sample", **common,
    )(o, sinks, qkv, qkv, qkv, cache_k, cache_v)


def _mem_kernel(q_ref, k_ref, v_ref, o_ref, *, scale, n_real):
    def compute():
        s = _dot_nt(q_ref[...].astype(_bf16), k_ref[...].astype(_bf16)) * scale
        e = jnp.exp(s - jnp.max(s, axis=-1, keepdims=True))
        w = e * (1.0 / jnp.sum(e, axis=-1, keepdims=True))
        o_ref[...] = _dot(w.astype(_bf16), v_ref[...].astype(_bf16))

    if n_real is None:
        compute()
    else:
        i = pl.program_id(1)
        pl.when(i < n_real)(compute)

        @pl.when(i >= n_real)
        def _():
            o_ref[...] = jnp.zeros(o_ref.shape, _f32)


def _mem_attend(q, kv_p, k_s, v_s, layer, *, t_prompt, n_batch_s, t_s, n_heads):
    n, d = q.shape
    dh = d // n_heads
    m = kv_p.shape[0]
    common = dict(out_shape=jax.ShapeDtypeStruct((n, d), _f32), compiler_params=_params("arbitrary", "arbitrary"))
    tq = _row_tile(t_prompt)
    nt = t_prompt // tq
    last = nt - 1
    o = pl.pallas_call(
        functools.partial(_mem_kernel, scale=dh ** -0.5, n_real=nt),
        grid=(n_heads, nt + _tail_steps(n, t_prompt, tq)),
        in_specs=[pl.BlockSpec((tq, dh), lambda h, i: (jnp.minimum(i, last), h)),
                  pl.BlockSpec((m, dh), lambda h, i: (0, h)),
                  pl.BlockSpec((m, dh), lambda h, i: (0, n_heads + h))],
        out_specs=pl.BlockSpec((tq, dh), lambda h, i: (i, h)),
        name="mem_prompt", **common,
    )(q, kv_p, kv_p)
    ob = t_prompt // t_s
    body = functools.partial(_mem_kernel, scale=dh ** -0.5, n_real=None)
    return pl.pallas_call(
        lambda full_ref, *refs: body(*refs),
        grid=(n_batch_s, n_heads),
        in_specs=[pl.BlockSpec(memory_space=pl.ANY),
                  pl.BlockSpec((t_s, dh), lambda b, h: (ob + b, h)),
                  pl.BlockSpec((None, None, m, dh), lambda b, h: (layer, b, 0, h)),
                  pl.BlockSpec((None, None, m, dh), lambda b, h: (layer, b, 0, h))],
        out_specs=pl.BlockSpec((t_s, dh), lambda b, h: (ob + b, h)),
        input_output_aliases={0: 0},
        name="mem_sample", **common,
    )(o, q, k_s, v_s)


def _router_kernel(x_ref, w_ref, b_ref, gate_ref, idx_ref):
    logits = jnp.dot(x_ref[...], w_ref[...], precision=lax.Precision.HIGHEST,
                     preferred_element_type=_f32) + b_ref[...]
    lane = lax.broadcasted_iota(jnp.int32, logits.shape, 1)
    lane_f = lane.astype(_f32)
    vals, ids = [], []
    for _ in range(TOP_K):
        m = jnp.max(logits, axis=-1, keepdims=True)
        am = jnp.min(jnp.where(logits == m, lane_f, float(LANE)), axis=-1, keepdims=True)
        vals.append(m)
        ids.append(am)
        logits = jnp.where(lane_f == am, -jnp.inf, logits)
    es = [jnp.exp(v - vals[0]) for v in vals]
    inv = 1.0 / sum(es)
    gates = jnp.zeros(logits.shape, _f32)
    idx = jnp.zeros(logits.shape, _f32)
    for k in range(TOP_K):
        gates = jnp.where(lane == k, es[k] * inv, gates)
        idx = jnp.where(lane == k, ids[k], idx)
    gate_ref[...] = gates
    idx_ref[...] = idx.astype(jnp.int32)


def _router(x, w_pad, b_pad):
    n, d = x.shape
    tm = _row_tile(n)
    return pl.pallas_call(
        _router_kernel,
        grid=(n // tm,),
        in_specs=[pl.BlockSpec((tm, d), lambda i: (i, 0)),
                  pl.BlockSpec((d, LANE), lambda i: (0, 0)),
                  pl.BlockSpec((1, LANE), lambda i: (0, 0))],
        out_specs=[pl.BlockSpec((tm, LANE), lambda i: (i, 0)), pl.BlockSpec((tm, LANE), lambda i: (i, 0))],
        out_shape=[jax.ShapeDtypeStruct((n, LANE), _f32), jax.ShapeDtypeStruct((n, LANE), jnp.int32)],
        compiler_params=_params("arbitrary"),
        name="router",
    )(x, w_pad, b_pad)


def _row_copy(src, s, dst, d, sem):
    return pltpu.make_async_copy(src.at[pl.ds(s, 1), :], dst.at[pl.ds(d, 1), :], sem)


def _scatter_kernel(dest_ref, x_ref, xs_in_ref, xs_ref, sem, *, tm):
    del xs_in_ref
    base = pl.program_id(0) * tm * TOP_K

    def issue(r, carry):
        for k in range(TOP_K):
            _row_copy(x_ref, r, xs_ref, dest_ref[base + r * TOP_K + k], sem).start()
        return carry

    def drain(r, carry):
        for k in range(TOP_K):
            _row_copy(x_ref, r, xs_ref, dest_ref[base + r * TOP_K + k], sem).wait()
        return carry

    lax.fori_loop(0, tm, issue, 0)
    lax.fori_loop(0, tm, drain, 0)


def _scatter_rows(x, dest, xs_buf):
    n, d = x.shape
    tm = _row_tile(n, 128)
    return pl.pallas_call(
        functools.partial(_scatter_kernel, tm=tm),
        grid_spec=pltpu.PrefetchScalarGridSpec(
            num_scalar_prefetch=1,
            grid=(n // tm,),
            in_specs=[pl.BlockSpec((tm, d), lambda i, dest: (i, 0)),
                      pl.BlockSpec(memory_space=pl.ANY)],
            out_specs=pl.BlockSpec(memory_space=pl.ANY),
            scratch_shapes=[pltpu.SemaphoreType.DMA(())]),
        out_shape=jax.ShapeDtypeStruct(xs_buf.shape, _f32),
        input_output_aliases={2: 0},
        compiler_params=_params("arbitrary"),
        name="moe_scatter",
    )(dest, x, xs_buf)


def _expert_kernel(sb_e_ref, sb_row_ref, sb_nblk_ref, sb_f_ref, xs_ref, wg_ref, wl_ref, bg_ref, bl_ref,
                   wd_ref, bd_ref, out_ref, xb_ref, acc_ref, stage_ref, wgb_ref, wlb_ref, wdb_ref,
                   pend_ref, sem_in, sem_out, *, nf):
    del sb_e_ref, sb_f_ref
    s = pl.program_id(0)
    f = pl.program_id(1)
    nblk = sb_nblk_ref[s]
    row0 = pl.multiple_of(sb_row_ref[s], MOE_BLK)
    max_blk = xb_ref.shape[0] // MOE_BLK
    n_stage = stage_ref.shape[0]

    def rows_of(j):
        return pl.ds(j * MOE_BLK, MOE_BLK)

    def in_copy(j):
        return pltpu.make_async_copy(xs_ref.at[pl.ds(row0 + j * MOE_BLK, MOE_BLK), :],
                                     stage_ref.at[j % n_stage], sem_in.at[j % n_stage])

    def out_copy(j):
        return pltpu.make_async_copy(acc_ref.at[rows_of(j), :],
                                     out_ref.at[pl.ds(row0 + j * MOE_BLK, MOE_BLK), :], sem_out.at[j])

    def drain_out():
        pending = pend_ref[0]
        for j in range(max_blk):
            @pl.when(j < pending)
            def _():
                out_copy(j).wait()
        pend_ref[0] = 0

    @pl.when((s == 0) & (f == 0))
    def _():
        pend_ref[0] = 0

    @pl.when((f == 0) & (nblk > 0))
    def _():
        for j in range(min(n_stage, max_blk)):
            @pl.when(j < nblk)
            def _():
                in_copy(j).start()
        for j in range(max_blk):
            @pl.when(j < nblk)
            def _():
                in_copy(j).wait()
                xb_ref[rows_of(j), :] = stage_ref[j % n_stage].astype(_bf16)
                if j + n_stage < max_blk:
                    @pl.when(j + n_stage < nblk)
                    def _():
                        in_copy(j + n_stage).start()
        drain_out()

    @pl.when(nblk > 0)
    def _():
        wgb_ref[...] = wg_ref[...].astype(_bf16)
        wlb_ref[...] = wl_ref[...].astype(_bf16)
        wdb_ref[...] = wd_ref[...].astype(_bf16)

        def block(j, carry, *, first):
            rows = pl.ds(pl.multiple_of(j * MOE_BLK, MOE_BLK), MOE_BLK)
            x = xb_ref[rows, :]
            gate = jnp.minimum(_dot(x, wgb_ref[...]) + bg_ref[...], SWIGLU_LIMIT)
            lin = jnp.clip(_dot(x, wlb_ref[...]) + bl_ref[...], -SWIGLU_LIMIT, SWIGLU_LIMIT)
            act = gate * jax.nn.sigmoid(SWIGLU_ALPHA * gate) * (lin + 1.0)
            part = _dot(act.astype(_bf16), wdb_ref[...])
            if first:
                acc_ref[rows, :] = part + bd_ref[...]
            else:
                acc_ref[rows, :] += part
            return carry

        @pl.when(f == 0)
        def _():
            lax.fori_loop(0, nblk, functools.partial(block, first=True), 0)

        @pl.when(f > 0)
        def _():
            lax.fori_loop(0, nblk, functools.partial(block, first=False), 0)

    @pl.when((f == nf - 1) & (nblk > 0))
    def _():
        for j in range(max_blk):
            @pl.when(j < nblk)
            def _():
                out_copy(j).start()
        pend_ref[0] = nblk

    @pl.when((s == pl.num_programs(0) - 1) & (f == nf - 1))
    def _():
        drain_out()


def _experts(xs, tables, w_gu, b_gu, w_d, b_d, layer):
    p, d = xs.shape
    n_e, _, ff2 = w_gu.shape[1:]
    ff = ff2 // 2
    ft = min(MOE_FT, ff)
    nf = ff // ft
    n_sb = tables[0].shape[0]
    b_gu3 = b_gu.reshape(b_gu.shape[0], n_e, 1, ff2)
    b_d3 = b_d.reshape(b_d.shape[0], n_e, 1, d)
    max_blk = MOE_SUPER // MOE_BLK
    return pl.pallas_call(
        functools.partial(_expert_kernel, nf=nf),
        grid_spec=pltpu.PrefetchScalarGridSpec(
            num_scalar_prefetch=4,
            grid=(n_sb, nf),
            in_specs=[pl.BlockSpec(memory_space=pl.ANY),
                      pl.BlockSpec((None, None, d, ft), lambda s, f, e, r, nb, fs: (layer, e[s], 0, fs[s * nf + f])),
                      pl.BlockSpec((None, None, d, ft), lambda s, f, e, r, nb, fs: (layer, e[s], 0, nf + fs[s * nf + f])),
                      pl.BlockSpec((None, None, 1, ft), lambda s, f, e, r, nb, fs: (layer, e[s], 0, fs[s * nf + f])),
                      pl.BlockSpec((None, None, 1, ft), lambda s, f, e, r, nb, fs: (layer, e[s], 0, nf + fs[s * nf + f])),
                      pl.BlockSpec((None, None, ft, d), lambda s, f, e, r, nb, fs: (layer, e[s], fs[s * nf + f], 0)),
                      pl.BlockSpec((None, None, 1, d), lambda s, f, e, r, nb, fs: (layer, e[s], 0, 0))],
            out_specs=pl.BlockSpec(memory_space=pl.ANY),
            scratch_shapes=[pltpu.VMEM((MOE_SUPER, d), _bf16), pltpu.VMEM((MOE_SUPER, d), _f32),
                            pltpu.VMEM((MOE_STAGE, MOE_BLK, d), _f32),
                            pltpu.VMEM((d, ft), _bf16), pltpu.VMEM((d, ft), _bf16), pltpu.VMEM((ft, d), _bf16),
                            pltpu.SMEM((1,), jnp.int32),
                            pltpu.SemaphoreType.DMA((MOE_STAGE,)),
                            pltpu.SemaphoreType.DMA((max_blk,))]),
        out_shape=jax.ShapeDtypeStruct((p, d), _f32),
        input_output_aliases={4: 0},
        compiler_params=_params("arbitrary", "arbitrary"),
        name="moe_experts",
    )(*tables, xs, w_gu, w_gu, b_gu3, b_gu3, w_d, b_d3)


def _combine_kernel(dest_ref, y_ref, gate_ref, res_ref, g_ref, be_ref, o_ref, buf_ref, sem, *, tm, alpha):
    base = pl.program_id(0) * tm * TOP_K

    def issue(r, carry):
        for k in range(TOP_K):
            _row_copy(y_ref, dest_ref[base + r * TOP_K + k], buf_ref.at[k], r, sem).start()
        return carry

    def drain(r, carry):
        for k in range(TOP_K):
            _row_copy(y_ref, dest_ref[base + r * TOP_K + k], buf_ref.at[k], r, sem).wait()
        return carry

    lax.fori_loop(0, tm, issue, 0)
    lax.fori_loop(0, tm, drain, 0)
    gates = gate_ref[...]
    mix = gates[:, 0:1] * buf_ref[0]
    for k in range(1, TOP_K):
        mix = mix + gates[:, k:k + 1] * buf_ref[k]
    o_ref[...] = _ln(alpha * res_ref[...] + mix, g_ref[...], be_ref[...])


def _combine(y, dest, gates, res, g, be, *, alpha):
    n, d = res.shape
    tm = _row_tile(n, 128)
    return pl.pallas_call(
        functools.partial(_combine_kernel, tm=tm, alpha=alpha),
        grid_spec=pltpu.PrefetchScalarGridSpec(
            num_scalar_prefetch=1,
            grid=(n // tm,),
            in_specs=[pl.BlockSpec(memory_space=pl.ANY),
                      pl.BlockSpec((tm, LANE), lambda i, dest: (i, 0)),
                      pl.BlockSpec((tm, d), lambda i, dest: (i, 0)),
                      pl.BlockSpec((1, d), lambda i, dest: (0, 0)),
                      pl.BlockSpec((1, d), lambda i, dest: (0, 0))],
            out_specs=pl.BlockSpec((tm, d), lambda i, dest: (i, 0)),
            scratch_shapes=[pltpu.VMEM((TOP_K, tm, d), _f32), pltpu.SemaphoreType.DMA(())]),
        out_shape=jax.ShapeDtypeStruct((n, d), _f32),
        compiler_params=_params("arbitrary"),
        name="moe_combine",
    )(dest, y, gates, res, g, be)


def _moe_tables(idx, n_e, n_sb, nf):
    flat_e = idx[:, :TOP_K].reshape(-1)
    onehot = (flat_e[:, None] == jnp.arange(n_e, dtype=jnp.int32)[None, :]).astype(jnp.int32)
    cum = jnp.cumsum(onehot, axis=0)
    counts = cum[-1]
    rank = jnp.sum(onehot * cum, axis=1) - 1
    padded = (counts + MOE_BLK - 1) // MOE_BLK * MOE_BLK
    pad_start = jnp.cumsum(padded) - padded
    dest = (pad_start[flat_e] + rank).astype(jnp.int32)

    n_super = (padded + MOE_SUPER - 1) // MOE_SUPER
    sb_end = jnp.cumsum(n_super)
    total = sb_end[-1]
    s = jnp.arange(n_sb, dtype=jnp.int32)
    valid = s < total
    s_c = jnp.minimum(s, total - 1)
    e = jnp.minimum(jnp.searchsorted(sb_end, s_c, side='right'), n_e - 1).astype(jnp.int32)
    j = s_c - (sb_end[e] - n_super[e])
    row0 = (pad_start[e] + j * MOE_SUPER).astype(jnp.int32)
    nblk = jnp.clip((padded[e] - j * MOE_SUPER) // MOE_BLK, 0, MOE_SUPER // MOE_BLK)
    nblk = jnp.where(valid, nblk, 0).astype(jnp.int32)
    f_idx = jnp.where(valid[:, None], jnp.arange(nf, dtype=jnp.int32)[None, :], nf - 1).reshape(-1).astype(jnp.int32)
    return dest, (e, row0, nblk, f_idx)


def kernel(x_prompt, x_sample, mem_prompt, state_conv, state_hgrn, cache_swa_k, cache_swa_v, cache_mem_k, cache_mem_v, ln_g, ln_b, conv_w_in, conv_b_in, conv_w_dw, conv_b_dw, conv_ln_g, conv_ln_b, conv_w_out, conv_b_out, hgrn_w_in, hgrn_lb, hgrn_norm_g, hgrn_w_out, swa_w_qkv, swa_b_qkv, swa_sinks, swa_w_out, swa_b_out, mem_w_q, mem_w_kv, mem_w_out, moe_w_router, moe_b_router, moe_w_gate_up, moe_b_gate_up, moe_w_down, moe_b_down):
    bp, tp, d = x_prompt.shape
    bs, ts, _ = x_sample.shape
    assert bp == 1, "the prompt group is one stream"
    depth = ln_g.shape[0]
    n = tp + bs * ts
    n_e = moe_w_router.shape[-1]
    n_kv = cache_swa_k.shape[3]
    keep = cache_swa_k.shape[2]
    mem_heads = cache_mem_k.shape[3]
    m_tok = mem_prompt.shape[1]
    kw = conv_w_dw.shape[1]
    conv_state = kw - 1
    nh = d // HGRN_HEAD_DIM
    alpha = (2 * depth) ** 0.25
    grp = dict(t_prompt=tp, n_batch_s=bs, t_s=ts)
    assert ts >= conv_state and keep == WINDOW_CHUNKS * CHUNK and n_e <= LANE

    def row(v):
        return v.reshape(1, -1).astype(_f32)

    zeros_d = jnp.zeros((1, d), _f32)
    lb_p = jax.nn.softmax(hgrn_lb.astype(_f32), axis=0)
    lower_bounds = jnp.cumsum(lb_p, axis=0) - lb_p[0]

    n_assign = n * TOP_K
    p_rows = ((n_assign + n_e * (MOE_BLK - 1)) // MOE_BLK + 1) * MOE_BLK
    n_sb = (n_assign + n_e * (MOE_BLK - 1)) // MOE_SUPER + n_e
    nf = moe_w_down.shape[2] // min(MOE_FT, moe_w_down.shape[2])
    xs_buf = jnp.zeros((p_rows, d), _f32)

    mem_k_all = cache_mem_k.reshape(depth, bs, m_tok, d)
    mem_v_all = cache_mem_v.reshape(depth, bs, m_tok, d)
    x = jnp.concatenate([x_prompt.reshape(tp, d), x_sample.reshape(bs * ts, d)], axis=0)
    p_conv, s_conv, p_hgrn, s_hgrn, p_k, p_v, s_k, s_v, p_mk, p_mv = ([] for _ in range(10))

    for i in range(depth):
        kind, slot = i % 3, i // 3
        if kind == 0:
            u = _mm_glu(x, conv_w_in[slot].astype(_bf16), row(conv_b_in[slot]))
            hist = jnp.pad(state_conv[slot], ((0, 0), (HIST_ROWS - conv_state, 0), (0, 0)))
            c = _conv(u, hist, conv_w_dw[slot], row(conv_b_dw[slot]), **grp)
            p_conv.append(u[tp - conv_state:tp][None])
            s_conv.append(u[tp:].reshape(bs, ts, d)[:, ts - conv_state:])
            x = _mm_res_ln(c, conv_w_out[slot].astype(_bf16), row(conv_b_out[slot]), x,
                           row(ln_g[i, 0]), row(ln_b[i, 0]), alpha=alpha,
                           pre_norm=(row(conv_ln_g[slot]), row(conv_ln_b[slot])))
        elif kind == 1:
            proj = _mm(x, hgrn_w_in[slot].astype(_bf16), jnp.zeros((1, 4 * d), _f32))
            lbv, ngv = row(lower_bounds[i]), row(hgrn_norm_g[slot])
            s0_p = jnp.zeros((1, nh, HGRN_HEAD_DIM, HGRN_HEAD_DIM), _f32)
            s0_s = jnp.swapaxes(state_hgrn[slot].astype(_f32), -1, -2)
            o, sf_p = _hgrn(proj, lbv, ngv, s0_p, None, row_off=0, n_batch=1, t_len=tp, name="hgrn_prompt")
            o, sf_s = _hgrn(proj, lbv, ngv, s0_s, o, row_off=tp, n_batch=bs, t_len=ts, name="hgrn_sample")
            p_hgrn.append(jnp.swapaxes(sf_p, -1, -2))
            s_hgrn.append(jnp.swapaxes(sf_s, -1, -2))
            x = _mm_res_ln(o, hgrn_w_out[slot].astype(_bf16), zeros_d, x,
                           row(ln_g[i, 0]), row(ln_b[i, 0]), alpha=alpha)
        else:
            qkv = _mm(x, swa_w_qkv[slot].astype(_bf16), row(swa_b_qkv[slot]), tn=swa_w_qkv.shape[-1])
            kvw = n_kv * SWA_HEAD_DIM
            qw = qkv.shape[1] - 2 * kvw
            ck = cache_swa_k[slot].reshape(bs, keep, kvw)
            cv = cache_swa_v[slot].reshape(bs, keep, kvw)
            o = _swa(qkv, swa_sinks[slot].astype(_f32), ck, cv, n_kv=n_kv, **grp)
            k_new, v_new = qkv[:, qw:qw + kvw], qkv[:, qw + kvw:]
            p_k.append(k_new[tp - keep:tp].reshape(1, keep, n_kv, SWA_HEAD_DIM))
            p_v.append(v_new[tp - keep:tp].reshape(1, keep, n_kv, SWA_HEAD_DIM))
            s_k.append(jnp.concatenate([ck, k_new[tp:].reshape(bs, ts, kvw)], axis=1)[:, -keep:]
                       .reshape(bs, keep, n_kv, SWA_HEAD_DIM))
            s_v.append(jnp.concatenate([cv, v_new[tp:].reshape(bs, ts, kvw)], axis=1)[:, -keep:]
                       .reshape(bs, keep, n_kv, SWA_HEAD_DIM))
            x = _mm_res_ln(o, swa_w_out[slot].astype(_bf16), row(swa_b_out[slot]), x,
                           row(ln_g[i, 0]), row(ln_b[i, 0]), alpha=alpha)

        kv_p = _mm(mem_prompt.reshape(m_tok, d), mem_w_kv[i].astype(_bf16), jnp.zeros((1, 2 * d), _f32))
        p_mk.append(kv_p[:, :d].reshape(1, m_tok, mem_heads, d // mem_heads))
        p_mv.append(kv_p[:, d:].reshape(1, m_tok, mem_heads, d // mem_heads))
        q = _mm(x, mem_w_q[i].astype(_bf16), zeros_d)
        o = _mem_attend(q, kv_p, mem_k_all, mem_v_all, i, n_heads=mem_heads, **grp)
        x = _mm_res_ln(o, mem_w_out[i].astype(_bf16), zeros_d, x, row(ln_g[i, 1]), row(ln_b[i, 1]), alpha=alpha)

        w_r = jnp.pad(moe_w_router[i].astype(_f32), ((0, 0), (0, LANE - n_e)))
        b_r = jnp.pad(moe_b_router[i].astype(_f32), (0, LANE - n_e), constant_values=NEG).reshape(1, LANE)
        gates, idx = _router(x, w_r, b_r)
        dest, tables = _moe_tables(idx, n_e, n_sb, nf)
        xs_buf = _scatter_rows(x, dest, xs_buf)
        xs_buf = _experts(xs_buf, tables, moe_w_gate_up, moe_b_gate_up, moe_w_down, moe_b_down, i)
        x = _combine(xs_buf, dest, gates, x, row(ln_g[i, 2]), row(ln_b[i, 2]), alpha=alpha)

    y_prompt = x[:tp].reshape(1, tp, d)
    y_sample = x[tp:].reshape(bs, ts, d)
    return (y_prompt, y_sample, jnp.stack(p_conv), jnp.stack(p_hgrn), jnp.stack(p_k), jnp.stack(p_v),
            jnp.stack(p_mk), jnp.stack(p_mv), jnp.stack(s_conv), jnp.stack(s_hgrn), jnp.stack(s_k), jnp.stack(s_v))
```

```python
import functools

import jax
import jax.numpy as jnp
from jax import lax
from jax.experimental import pallas as pl
from jax.experimental.pallas import tpu as pltpu

PAST_LEN = 1024
CHUNK = 64
WINDOW_CHUNKS = 2
SWA_HEAD_DIM = 64
HGRN_HEAD_DIM = 128
TOP_K = 4
SWIGLU_LIMIT = 7.0
SWIGLU_ALPHA = 1.702
LN_EPS = 1e-5

LANE = 128
SUBLANE = 8
VMEM_LIMIT = 56 * 1024 * 1024
HIST_ROWS = 32
MOE_BLK = 256
MOE_SUPER = 1280
MOE_STAGE = 2
MOE_FT = 512
MOE_PAIR = 2
DMA_PRIORITIES = 2
HGRN_SUB = 16
HGRN_HEADS_PER_STEP = 4
NEG = -1e30

_bf16 = jnp.bfloat16
_f32 = jnp.float32


def _params(*sem):
    return pltpu.CompilerParams(dimension_semantics=sem, vmem_limit_bytes=VMEM_LIMIT)


def _row_tile(n, cap=256):
    for t in (cap, 128, 64, 32, 16, 8):
        if t <= cap and n % t == 0:
            return t
    raise ValueError(f"row count {n} is not a multiple of {SUBLANE}")


def _tail_steps(n, n_own, tile):
    tail = n - n_own
    assert tail % tile == 0, "rows after a call's own must fill whole tiles"
    return tail // tile


def _ln(y, g, b):
    mu = jnp.mean(y, axis=-1, keepdims=True)
    d = y - mu
    var = jnp.mean(d * d, axis=-1, keepdims=True)
    return d * lax.rsqrt(var + LN_EPS) * g + b


def _dot(a, b):
    return jnp.dot(a, b, preferred_element_type=_f32)


def _dot_nt(a, b):
    return lax.dot_general(a, b, (((1,), (1,)), ((), ())), preferred_element_type=_f32)


def _mm_kernel(x_ref, w_ref, b_ref, o_ref):
    o_ref[...] = _dot(x_ref[...].astype(_bf16), w_ref[...]) + b_ref[...]


def _mm(x, w, b, *, tn=2048):
    m, k = x.shape
    n = w.shape[1]
    tn = min(tn, n)
    tm = _row_tile(m)
    return pl.pallas_call(
        _mm_kernel,
        grid=(n // tn, m // tm),
        in_specs=[pl.BlockSpec((tm, k), lambda j, i: (i, 0)),
                  pl.BlockSpec((k, tn), lambda j, i: (0, j)),
                  pl.BlockSpec((1, tn), lambda j, i: (0, j))],
        out_specs=pl.BlockSpec((tm, tn), lambda j, i: (i, j)),
        out_shape=jax.ShapeDtypeStruct((m, n), _f32),
        compiler_params=_params("arbitrary", "arbitrary"),
        name="mm_bias",
    )(x, w, b)


def _mm_glu_kernel(x_ref, wa_ref, wb_ref, ba_ref, bb_ref, o_ref):
    x = x_ref[...].astype(_bf16)
    a = _dot(x, wa_ref[...]) + ba_ref[...]
    g = _dot(x, wb_ref[...]) + bb_ref[...]
    o_ref[...] = a * jax.nn.sigmoid(g)


def _mm_glu(x, w, b, *, tn=1024):
    m, k = x.shape
    n = w.shape[1] // 2
    tn = min(tn, n)
    tm = _row_tile(m)
    nj = n // tn
    return pl.pallas_call(
        _mm_glu_kernel,
        grid=(nj, m // tm),
        in_specs=[pl.BlockSpec((tm, k), lambda j, i: (i, 0)),
                  pl.BlockSpec((k, tn), lambda j, i: (0, j)),
                  pl.BlockSpec((k, tn), lambda j, i: (0, j + nj)),
                  pl.BlockSpec((1, tn), lambda j, i: (0, j)),
                  pl.BlockSpec((1, tn), lambda j, i: (0, j + nj))],
        out_specs=pl.BlockSpec((tm, tn), lambda j, i: (i, j)),
        out_shape=jax.ShapeDtypeStruct((m, n), _f32),
        compiler_params=_params("arbitrary", "arbitrary"),
        name="mm_glu",
    )(x, w, w, b, b)


def _mm_res_ln_kernel(x_ref, w_ref, b_ref, res_ref, g_ref, be_ref, o_ref, *, alpha):
    h = _dot(x_ref[...].astype(_bf16), w_ref[...]) + b_ref[...]
    o_ref[...] = _ln(alpha * res_ref[...] + h, g_ref[...], be_ref[...])


def _mm_norm_res_ln_kernel(x_ref, ng_ref, nb_ref, w_ref, b_ref, res_ref, g_ref, be_ref, o_ref, *, alpha):
    c = _ln(x_ref[...], ng_ref[...], nb_ref[...])
    c = c * jax.nn.sigmoid(c)
    h = _dot(c.astype(_bf16), w_ref[...]) + b_ref[...]
    o_ref[...] = _ln(alpha * res_ref[...] + h, g_ref[...], be_ref[...])


def _mm_res_ln(x, w, b, res, g, be, *, alpha, pre_norm=None):
    m, k = x.shape
    n = w.shape[1]
    tm = _row_tile(m)
    row = lambda i: (i, 0)
    fix = lambda i: (0, 0)
    specs = [pl.BlockSpec((tm, k), row)]
    args = [x]
    if pre_norm is not None:
        specs += [pl.BlockSpec((1, k), fix), pl.BlockSpec((1, k), fix)]
        args += list(pre_norm)
        body = _mm_norm_res_ln_kernel
    else:
        body = _mm_res_ln_kernel
    specs += [pl.BlockSpec((k, n), fix), pl.BlockSpec((1, n), fix), pl.BlockSpec((tm, n), row),
              pl.BlockSpec((1, n), fix), pl.BlockSpec((1, n), fix)]
    args += [w, b, res, g, be]
    return pl.pallas_call(
        functools.partial(body, alpha=alpha),
        grid=(m // tm,),
        in_specs=specs,
        out_specs=pl.BlockSpec((tm, n), row),
        out_shape=jax.ShapeDtypeStruct((m, n), _f32),
        compiler_params=_params("arbitrary"),
        name="mm_res_ln",
    )(*args)


def _conv_kernel(hist_ref, u_ref, w_ref, b_ref, o_ref, buf_ref, *, n_real, tt, lb, kw):
    i = pl.program_id(0)

    def compute():
        hist = hist_ref[...]
        if n_real is not None:
            hist = jnp.where(i == 0, 0.0, hist)
        buf_ref[0:HIST_ROWS, :] = hist
        buf_ref[HIST_ROWS:, :] = u_ref[...]
        off = HIST_ROWS - (kw - 1)
        rb = min(tt, 64)
        for r in range(tt // rb):
            for c in range(lb // LANE):
                cs = slice(c * LANE, (c + 1) * LANE)
                acc = jnp.zeros((rb, LANE), _f32)
                for j in range(kw):
                    acc = acc + w_ref[j:j + 1, cs] * buf_ref[r * rb + off + j:r * rb + off + j + rb, cs]
                o_ref[r * rb:(r + 1) * rb, cs] = acc + b_ref[:, cs]

    if n_real is None:
        compute()
    else:
        pl.when(i < n_real)(compute)

        @pl.when(i >= n_real)
        def _():
            o_ref[...] = jnp.zeros(o_ref.shape, _f32)


def _conv(u, hist_s, w, b, *, t_prompt, n_batch_s, t_s):
    n, d = u.shape
    kw = w.shape[0]
    lb = min(512, d)
    tt = _row_tile(t_prompt)
    hb = tt // HIST_ROWS
    nt = t_prompt // tt
    last = nt - 1
    common = dict(out_shape=jax.ShapeDtypeStruct((n, d), _f32), compiler_params=_params("arbitrary", "arbitrary"))
    c = pl.pallas_call(
        functools.partial(_conv_kernel, n_real=nt, tt=tt, lb=lb, kw=kw),
        grid=(nt + _tail_steps(n, t_prompt, tt), d // lb),
        in_specs=[pl.BlockSpec((HIST_ROWS, lb), lambda i, j: (jnp.maximum(jnp.minimum(i, last) * hb - 1, 0), j)),
                  pl.BlockSpec((tt, lb), lambda i, j: (jnp.minimum(i, last), j)),
                  pl.BlockSpec((kw, lb), lambda i, j: (0, j)),
                  pl.BlockSpec((1, lb), lambda i, j: (0, j))],
        out_specs=pl.BlockSpec((tt, lb), lambda i, j: (i, j)),
        scratch_shapes=[pltpu.VMEM((HIST_ROWS + tt, lb), _f32)],
        name="conv_prompt", **common,
    )(u, u, w, b)
    ob = t_prompt // t_s
    return pl.pallas_call(
        lambda full_ref, *refs: _conv_kernel(*refs, n_real=None, tt=t_s, lb=lb, kw=kw),
        grid=(n_batch_s, d // lb),
        in_specs=[pl.BlockSpec(memory_space=pl.ANY),
                  pl.BlockSpec((None, HIST_ROWS, lb), lambda i, j: (i, 0, j)),
                  pl.BlockSpec((t_s, lb), lambda i, j: (ob + i, j)),
                  pl.BlockSpec((kw, lb), lambda i, j: (0, j)),
                  pl.BlockSpec((1, lb), lambda i, j: (0, j))],
        out_specs=pl.BlockSpec((t_s, lb), lambda i, j: (ob + i, j)),
        scratch_shapes=[pltpu.VMEM((HIST_ROWS + t_s, lb), _f32)],
        input_output_aliases={0: 0},
        name="conv_sample", **common,
    )(c, hist_s, u, w, b)


def _hgrn_kernel(q_ref, f_ref, i_ref, og_ref, lb_ref, ng_ref, s0_ref, o_ref, sf_ref, st_ref, *,
                 tt, blk, heads, n_real):
    t = pl.program_id(2)
    hd = HGRN_HEAD_DIM

    @pl.when(t == 0)
    def _():
        st_ref[...] = s0_ref[...]

    rr = lax.broadcasted_iota(jnp.int32, (blk, blk), 0)
    cc = lax.broadcasted_iota(jnp.int32, (blk, blk), 1)
    tril = (cc <= rr).astype(_f32)
    sub_row = lax.broadcasted_iota(jnp.int32, (HGRN_SUB, LANE), 0)

    def head_chunk(base, hh):
        rows = pl.ds(base, blk)
        ls = slice(hh * hd, (hh + 1) * hd)
        lbv = lb_ref[:, ls]
        q = q_ref[rows, ls]
        qh = q * jax.nn.sigmoid(q)
        forget = lbv + (1.0 - lbv) * jax.nn.sigmoid(f_ref[rows, ls])
        kh = 1.0 - forget
        v = i_ref[rows, ls]
        b = jnp.dot(tril, jnp.log(forget), precision=lax.Precision.HIGHEST, preferred_element_type=_f32)
        st = st_ref[hh]
        inter = _dot_nt((qh * jnp.exp(b)).astype(_bf16), st.astype(_bf16))
        b_last = b[blk - 1:blk, :]
        kf_end = (kh * jnp.exp(b_last - b)).astype(_bf16)
        st_ref[hh] = st * jnp.exp(b_last) + lax.dot_general(
            v.astype(_bf16), kf_end, (((0,), (0,)), ((), ())), preferred_element_type=_f32)
        vb = v.astype(_bf16)
        for blk_i in range(blk // HGRN_SUB):
            r0 = blk_i * HGRN_SUB
            b_i = b[r0:r0 + HGRN_SUB, :]
            q_i = qh[r0:r0 + HGRN_SUB, :]
            acc = inter[r0:r0 + HGRN_SUB, :]
            for s in range(HGRN_SUB):
                keep = sub_row >= s
                decay = jnp.exp(jnp.where(keep, b_i - b[r0 + s:r0 + s + 1, :], NEG))
                col = jnp.sum(q_i * decay * kh[r0 + s:r0 + s + 1, :], axis=-1, keepdims=True)
                acc = acc + col * v[r0 + s:r0 + s + 1, :]
            if r0 > 0:
                anchor = b[r0 - 1:r0, :]
                qf = (q_i * jnp.exp(b_i - anchor)).astype(_bf16)
                kf = (kh[0:r0, :] * jnp.exp(anchor - b[0:r0, :])).astype(_bf16)
                acc = acc + _dot(_dot_nt(qf, kf).astype(_bf16), vb[0:r0, :])
            o = acc * lax.rsqrt(jnp.mean(acc * acc, axis=-1, keepdims=True) + LN_EPS)
            o_rows = pl.ds(base + r0, HGRN_SUB)
            o_ref[o_rows, ls] = o * ng_ref[:, ls] * jax.nn.sigmoid(og_ref[o_rows, ls])

    def chunk(c, carry):
        base = pl.multiple_of(c * blk, blk)
        for hh in range(heads):
            head_chunk(base, hh)
        return carry

    @pl.when(t < n_real)
    def _():
        lax.fori_loop(0, tt // blk, chunk, 0)

    @pl.when(t >= n_real)
    def _():
        o_ref[...] = jnp.zeros(o_ref.shape, _f32)

    @pl.when(t == n_real - 1)
    def _():
        sf_ref[...] = st_ref[...]


def _hgrn(proj, lb, ng, s0_t, o_prev, *, row_off, n_batch, t_len, name):
    n, d4 = proj.shape
    d = d4 // 4
    hd = HGRN_HEAD_DIM
    nh = d // hd
    heads = HGRN_HEADS_PER_STEP if nh % HGRN_HEADS_PER_STEP == 0 else 1
    n_groups = nh // heads
    w = heads * hd
    blk = min(t_len, CHUNK)
    tt = _row_tile(t_len)
    nt = t_len // tt
    last = nt - 1
    ob = row_off // tt
    tail = _tail_steps(n, row_off + n_batch * t_len, tt) if o_prev is None else 0
    row = lambda k: (lambda b, h, t: (ob + b * nt + jnp.minimum(t, last), k * n_groups + h))
    head = lambda b, h, t: (0, h)
    in_specs = [pl.BlockSpec((tt, w), row(0)), pl.BlockSpec((tt, w), row(1)),
                pl.BlockSpec((tt, w), row(2)), pl.BlockSpec((tt, w), row(3)),
                pl.BlockSpec((1, w), head), pl.BlockSpec((1, w), head),
                pl.BlockSpec((None, heads, hd, hd), lambda b, h, t: (b, h, 0, 0))]
    args = [proj, proj, proj, proj, lb, ng, s0_t]
    body = functools.partial(_hgrn_kernel, tt=tt, blk=blk, heads=heads, n_real=nt)
    aliases = {}
    if o_prev is not None:
        in_specs = [pl.BlockSpec(memory_space=pl.ANY)] + in_specs
        args = [o_prev] + args
        inner = body
        body = lambda full_ref, *refs: inner(*refs)
        aliases = {0: 0}
    return pl.pallas_call(
        body,
        grid=(n_batch, n_groups, nt + tail),
        in_specs=in_specs,
        out_specs=[pl.BlockSpec((tt, w), lambda b, h, t: (ob + b * nt + t, h)),
                   pl.BlockSpec((None, heads, hd, hd), lambda b, h, t: (b, h, 0, 0))],
        out_shape=[jax.ShapeDtypeStruct((n, d), _f32),
                   jax.ShapeDtypeStruct((n_batch, nh, hd, hd), _f32)],
        scratch_shapes=[pltpu.VMEM((heads, hd, hd), _f32)],
        input_output_aliases=aliases,
        compiler_params=_params("arbitrary", "arbitrary", "arbitrary"),
        name=name,
    )(*args)


def _swa_kernel(sink_ref, q_ref, kc_ref, vc_ref, kp_ref, vp_ref, o_ref, *, tq, win, pos0, n_kv, group, n_real):
    dh = SWA_HEAD_DIM
    i = pl.program_id(1)
    n_heads = n_kv * group

    def compute():
        qpos0 = pos0 + i * tq

        def bias_terms(nk, kpos0):
            qp = qpos0 + lax.broadcasted_iota(jnp.int32, (tq, nk), 0)
            kp = kpos0 + lax.broadcasted_iota(jnp.int32, (tq, nk), 1)
            qc = qp // CHUNK
            kc = jnp.maximum(kp, 0) // CHUNK
            vis = (kp >= 0) & (kc <= qc) & (qc - kc <= WINDOW_CHUNKS)
            return vis, jnp.abs(qp - kp).astype(_f32)

        vis_p, dist_p = bias_terms(win, qpos0 - win)
        vis_c, dist_c = bias_terms(tq, qpos0)
        scale = dh ** -0.5
        for kv in range(n_kv):
            ks = slice(kv * dh, (kv + 1) * dh)
            kp = kp_ref[:, ks].astype(_bf16)
            kc = kc_ref[:, ks].astype(_bf16)
            vp = vp_ref[:, ks].astype(_bf16)
            vc = vc_ref[:, ks].astype(_bf16)
            heads = [kv * group + g for g in range(group)]
            qs = jnp.concatenate([q_ref[:, h * dh:(h + 1) * dh] for h in heads], axis=0).astype(_bf16)
            raw_p = _dot_nt(qs, kp)
            raw_c = _dot_nt(qs, kc)
            w_p, w_c = [], []
            for g, h in enumerate(heads):
                slope = 2.0 ** (-8.0 * (h + 1) / n_heads)
                rs = slice(g * tq, (g + 1) * tq)
                s_p = jnp.where(vis_p, raw_p[rs] * scale - slope * dist_p, NEG)
                s_c = jnp.where(vis_c, raw_c[rs] * scale - slope * dist_c, NEG)
                sink = sink_ref[h]
                m = jnp.maximum(jnp.maximum(jnp.max(s_p, axis=-1, keepdims=True),
                                            jnp.max(s_c, axis=-1, keepdims=True)), sink)
                e_p = jnp.exp(s_p - m)
                e_c = jnp.exp(s_c - m)
                den = (jnp.sum(e_p, axis=-1, keepdims=True) + jnp.sum(e_c, axis=-1, keepdims=True)
                       + jnp.exp(sink - m))
                inv = 1.0 / den
                w_p.append((e_p * inv).astype(_bf16))
                w_c.append((e_c * inv).astype(_bf16))
            o = _dot(jnp.concatenate(w_p, axis=0), vp) + _dot(jnp.concatenate(w_c, axis=0), vc)
            for g, h in enumerate(heads):
                o_ref[:, h * dh:(h + 1) * dh] = o[g * tq:(g + 1) * tq]

    if n_real is None:
        compute()
    else:
        pl.when(i < n_real)(compute)

        @pl.when(i >= n_real)
        def _():
            o_ref[...] = jnp.zeros(o_ref.shape, _f32)


def _swa(qkv, sinks, cache_k, cache_v, *, t_prompt, n_batch_s, t_s, n_kv):
    n, width = qkv.shape
    dh = SWA_HEAD_DIM
    kvw = n_kv * dh
    qw = width - 2 * kvw
    group = qw // kvw
    win = WINDOW_CHUNKS * CHUNK
    kcol, vcol = qw // kvw, qw // kvw + 1
    common = dict(out_shape=jax.ShapeDtypeStruct((n, qw), _f32), compiler_params=_params("arbitrary", "arbitrary"))
    smem = pl.BlockSpec(memory_space=pltpu.SMEM)
    tq = win
    nt = t_prompt // tq
    last = nt - 1
    cur = lambda c: (lambda b, i: (jnp.minimum(i, last), c))
    prev = lambda c: (lambda b, i: (jnp.maximum(jnp.minimum(i, last) - 1, 0), c))
    o = pl.pallas_call(
        functools.partial(_swa_kernel, tq=tq, win=win, pos0=0, n_kv=n_kv, group=group, n_real=nt),
        grid=(1, nt + _tail_steps(n, t_prompt, tq)),
        in_specs=[smem,
                  pl.BlockSpec((tq, qw), cur(0)),
                  pl.BlockSpec((tq, kvw), cur(kcol)),
                  pl.BlockSpec((tq, kvw), cur(vcol)),
                  pl.BlockSpec((win, kvw), prev(kcol)),
                  pl.BlockSpec((win, kvw), prev(vcol))],
        out_specs=pl.BlockSpec((tq, qw), lambda b, i: (i, 0)),
        name="swa_prompt", **common,
    )(sinks, qkv, qkv, qkv, qkv, qkv)
    ob = t_prompt // t_s
    inner = functools.partial(_swa_kernel, tq=t_s, win=win, pos0=PAST_LEN, n_kv=n_kv, group=group, n_real=None)
    return pl.pallas_call(
        lambda full_ref, *refs: inner(*refs),
        grid=(n_batch_s, 1),
        in_specs=[pl.BlockSpec(memory_space=pl.ANY), smem,
                  pl.BlockSpec((t_s, qw), lambda b, i: (ob + b, 0)),
                  pl.BlockSpec((t_s, kvw), lambda b, i: (ob + b, kcol)),
                  pl.BlockSpec((t_s, kvw), lambda b, i: (ob + b, vcol)),
                  pl.BlockSpec((None, win, kvw), lambda b, i: (b, 0, 0)),
                  pl.BlockSpec((None, win, kvw), lambda b, i: (b, 0, 0))],
        out_specs=pl.BlockSpec((t_s, qw), lambda b, i: (ob + b, 0)),
        input_output_aliases={0: 0},
        name="swa_sample", **common,
    )(o, sinks, qkv, qkv, qkv, cache_k, cache_v)


def _mem_kernel(q_ref, k_ref, v_ref, o_ref, *, scale, n_real):
    def compute():
        s = _dot_nt(q_ref[...].astype(_bf16), k_ref[...].astype(_bf16)) * scale
        e = jnp.exp(s - jnp.max(s, axis=-1, keepdims=True))
        w = e * (1.0 / jnp.sum(e, axis=-1, keepdims=True))
        o_ref[...] = _dot(w.astype(_bf16), v_ref[...].astype(_bf16))

    if n_real is None:
        compute()
    else:
        i = pl.program_id(1)
        pl.when(i < n_real)(compute)

        @pl.when(i >= n_real)
        def _():
            o_ref[...] = jnp.zeros(o_ref.shape, _f32)


def _mem_attend(q, kv_p, k_s, v_s, layer, *, t_prompt, n_batch_s, t_s, n_heads):
    n, d = q.shape
    dh = d // n_heads
    m = kv_p.shape[0]
    common = dict(out_shape=jax.ShapeDtypeStruct((n, d), _f32), compiler_params=_params("arbitrary", "arbitrary"))
    tq = _row_tile(t_prompt)
    nt = t_prompt // tq
    last = nt - 1
    o = pl.pallas_call(
        functools.partial(_mem_kernel, scale=dh ** -0.5, n_real=nt),
        grid=(n_heads, nt + _tail_steps(n, t_prompt, tq)),
        in_specs=[pl.BlockSpec((tq, dh), lambda h, i: (jnp.minimum(i, last), h)),
                  pl.BlockSpec((m, dh), lambda h, i: (0, h)),
                  pl.BlockSpec((m, dh), lambda h, i: (0, n_heads + h))],
        out_specs=pl.BlockSpec((tq, dh), lambda h, i: (i, h)),
        name="mem_prompt", **common,
    )(q, kv_p, kv_p)
    ob = t_prompt // t_s

    def sample_body(full_ref, q_ref, k_ref, v_ref, o_ref):
        del full_ref
        for h in range(n_heads):
            hs = slice(h * dh, (h + 1) * dh)
            s = _dot_nt(q_ref[:, hs].astype(_bf16), k_ref[:, h, :].astype(_bf16)) * dh ** -0.5
            e = jnp.exp(s - jnp.max(s, axis=-1, keepdims=True))
            w = e * (1.0 / jnp.sum(e, axis=-1, keepdims=True))
            o_ref[:, hs] = _dot(w.astype(_bf16), v_ref[:, h, :].astype(_bf16))

    cache_spec = pl.BlockSpec((None, None, m, n_heads, dh), lambda b: (layer, b, 0, 0, 0))
    return pl.pallas_call(
        sample_body,
        grid=(n_batch_s,),
        in_specs=[pl.BlockSpec(memory_space=pl.ANY),
                  pl.BlockSpec((t_s, d), lambda b: (ob + b, 0)),
                  cache_spec, cache_spec],
        out_specs=pl.BlockSpec((t_s, d), lambda b: (ob + b, 0)),
        out_shape=jax.ShapeDtypeStruct((n, d), _f32),
        input_output_aliases={0: 0},
        compiler_params=_params("arbitrary"),
        name="mem_sample",
    )(o, q, k_s, v_s)


def _router_kernel(x_ref, w_ref, b_ref, gate_ref, idx_ref):
    logits = jnp.dot(x_ref[...], w_ref[...], precision=lax.Precision.HIGHEST,
                     preferred_element_type=_f32) + b_ref[...]
    lane = lax.broadcasted_iota(jnp.int32, logits.shape, 1)
    lane_f = lane.astype(_f32)
    vals, ids = [], []
    for _ in range(TOP_K):
        m = jnp.max(logits, axis=-1, keepdims=True)
        am = jnp.min(jnp.where(logits == m, lane_f, float(LANE)), axis=-1, keepdims=True)
        vals.append(m)
        ids.append(am)
        logits = jnp.where(lane_f == am, -jnp.inf, logits)
    es = [jnp.exp(v - vals[0]) for v in vals]
    inv = 1.0 / sum(es)
    gates = jnp.zeros(logits.shape, _f32)
    idx = jnp.zeros(logits.shape, _f32)
    for k in range(TOP_K):
        gates = jnp.where(lane == k, es[k] * inv, gates)
        idx = jnp.where(lane == k, ids[k], idx)
    gate_ref[...] = gates
    idx_ref[...] = idx.astype(jnp.int32)


def _router(x, w_pad, b_pad):
    n, d = x.shape
    tm = _row_tile(n)
    return pl.pallas_call(
        _router_kernel,
        grid=(n // tm,),
        in_specs=[pl.BlockSpec((tm, d), lambda i: (i, 0)),
                  pl.BlockSpec((d, LANE), lambda i: (0, 0)),
                  pl.BlockSpec((1, LANE), lambda i: (0, 0))],
        out_specs=[pl.BlockSpec((tm, LANE), lambda i: (i, 0)), pl.BlockSpec((tm, LANE), lambda i: (i, 0))],
        out_shape=[jax.ShapeDtypeStruct((n, LANE), _f32), jax.ShapeDtypeStruct((n, LANE), jnp.int32)],
        compiler_params=_params("arbitrary"),
        name="router",
    )(x, w_pad, b_pad)


def _row_copy(src, s, dst, d, sem):
    return pltpu.make_async_copy(src.at[pl.ds(s, 1), :], dst.at[pl.ds(d, 1), :], sem)


def _scatter_kernel(dest_ref, x_ref, xs_in_ref, xs_ref, sem, *, tm):
    del xs_in_ref
    base = pl.program_id(0) * tm * TOP_K

    def issue(r, carry):
        for k in range(TOP_K):
            _row_copy(x_ref, r, xs_ref, dest_ref[base + r * TOP_K + k], sem).start(priority=k % DMA_PRIORITIES)
        return carry

    def drain(r, carry):
        for k in range(TOP_K):
            _row_copy(x_ref, r, xs_ref, dest_ref[base + r * TOP_K + k], sem).wait()
        return carry

    lax.fori_loop(0, tm, issue, 0)
    lax.fori_loop(0, tm, drain, 0)


def _scatter_rows(x, dest, xs_buf):
    n, d = x.shape
    tm = _row_tile(n, 128)
    return pl.pallas_call(
        functools.partial(_scatter_kernel, tm=tm),
        grid_spec=pltpu.PrefetchScalarGridSpec(
            num_scalar_prefetch=1,
            grid=(n // tm,),
            in_specs=[pl.BlockSpec((tm, d), lambda i, dest: (i, 0)),
                      pl.BlockSpec(memory_space=pl.ANY)],
            out_specs=pl.BlockSpec(memory_space=pl.ANY),
            scratch_shapes=[pltpu.SemaphoreType.DMA(())]),
        out_shape=jax.ShapeDtypeStruct(xs_buf.shape, _f32),
        input_output_aliases={2: 0},
        compiler_params=_params("arbitrary"),
        name="moe_scatter",
    )(dest, x, xs_buf)


def _expert_kernel(sb_e_ref, sb_row_ref, sb_nblk_ref, sb_f_ref, xs_ref, wg_ref, wl_ref, bg_ref, bl_ref,
                   wd_ref, bd_ref, out_ref, xb_ref, acc_ref, stage_ref, wgb_ref, wlb_ref, wdb_ref,
                   pend_ref, sem_in, sem_out, *, nf):
    del sb_e_ref, sb_f_ref
    s = pl.program_id(0)
    f = pl.program_id(1)
    nblk = sb_nblk_ref[s]
    row0 = pl.multiple_of(sb_row_ref[s], MOE_BLK)
    max_blk = xb_ref.shape[0] // MOE_BLK
    n_stage = stage_ref.shape[0]

    def rows_of(j):
        return pl.ds(j * MOE_BLK, MOE_BLK)

    def in_copy(j):
        return pltpu.make_async_copy(xs_ref.at[pl.ds(row0 + j * MOE_BLK, MOE_BLK), :],
                                     stage_ref.at[j % n_stage], sem_in.at[j % n_stage])

    def out_copy(j):
        return pltpu.make_async_copy(acc_ref.at[rows_of(j), :],
                                     out_ref.at[pl.ds(row0 + j * MOE_BLK, MOE_BLK), :], sem_out.at[j])

    def drain_out():
        pending = pend_ref[0]
        for j in range(max_blk):
            @pl.when(j < pending)
            def _():
                out_copy(j).wait()
        pend_ref[0] = 0

    @pl.when((s == 0) & (f == 0))
    def _():
        pend_ref[0] = 0

    @pl.when((f == 0) & (nblk > 0))
    def _():
        for j in range(min(n_stage, max_blk)):
            @pl.when(j < nblk)
            def _():
                in_copy(j).start()
        for j in range(max_blk):
            @pl.when(j < nblk)
            def _():
                in_copy(j).wait()
                xb_ref[rows_of(j), :] = stage_ref[j % n_stage].astype(_bf16)
                if j + n_stage < max_blk:
                    @pl.when(j + n_stage < nblk)
                    def _():
                        in_copy(j + n_stage).start()
        drain_out()

    @pl.when(nblk > 0)
    def _():
        wgb_ref[...] = wg_ref[...].astype(_bf16)
        wlb_ref[...] = wl_ref[...].astype(_bf16)
        wdb_ref[...] = wd_ref[...].astype(_bf16)

        def rows_mlp(rows, first):
            x = xb_ref[rows, :]
            gate = jnp.minimum(_dot(x, wgb_ref[...]) + bg_ref[...], SWIGLU_LIMIT)
            lin = jnp.clip(_dot(x, wlb_ref[...]) + bl_ref[...], -SWIGLU_LIMIT, SWIGLU_LIMIT)
            act = gate * jax.nn.sigmoid(SWIGLU_ALPHA * gate) * (lin + 1.0)
            part = _dot(act.astype(_bf16), wdb_ref[...])
            if first:
                acc_ref[rows, :] = part + bd_ref[...]
            else:
                acc_ref[rows, :] += part

        def sweep(first):
            group = MOE_PAIR * MOE_BLK
            n_group = nblk // MOE_PAIR

            def trip(j, carry):
                rows_mlp(pl.ds(pl.multiple_of(j * group, group), group), first)
                return carry

            lax.fori_loop(0, n_group, trip, 0)
            for extra in range(MOE_PAIR - 1):
                @pl.when(n_group * MOE_PAIR + extra < nblk)
                def _():
                    start = pl.multiple_of((n_group * MOE_PAIR + extra) * MOE_BLK, MOE_BLK)
                    rows_mlp(pl.ds(start, MOE_BLK), first)

        @pl.when(f == 0)
        def _():
            sweep(True)

        @pl.when(f > 0)
        def _():
            sweep(False)

    @pl.when((f == nf - 1) & (nblk > 0))
    def _():
        for j in range(max_blk):
            @pl.when(j < nblk)
            def _():
                out_copy(j).start()
        pend_ref[0] = nblk

    @pl.when((s == pl.num_programs(0) - 1) & (f == nf - 1))
    def _():
        drain_out()


def _experts(xs, tables, w_gu, b_gu, w_d, b_d, layer):
    p, d = xs.shape
    n_e, _, ff2 = w_gu.shape[1:]
    ff = ff2 // 2
    ft = min(MOE_FT, ff)
    nf = ff // ft
    n_sb = tables[0].shape[0]
    b_gu3 = b_gu.reshape(b_gu.shape[0], n_e, 1, ff2)
    b_d3 = b_d.reshape(b_d.shape[0], n_e, 1, d)
    max_blk = MOE_SUPER // MOE_BLK
    return pl.pallas_call(
        functools.partial(_expert_kernel, nf=nf),
        grid_spec=pltpu.PrefetchScalarGridSpec(
            num_scalar_prefetch=4,
            grid=(n_sb, nf),
            in_specs=[pl.BlockSpec(memory_space=pl.ANY),
                      pl.BlockSpec((None, None, d, ft), lambda s, f, e, r, nb, fs: (layer, e[s], 0, fs[s * nf + f])),
                      pl.BlockSpec((None, None, d, ft), lambda s, f, e, r, nb, fs: (layer, e[s], 0, nf + fs[s * nf + f])),
                      pl.BlockSpec((None, None, 1, ft), lambda s, f, e, r, nb, fs: (layer, e[s], 0, fs[s * nf + f])),
                      pl.BlockSpec((None, None, 1, ft), lambda s, f, e, r, nb, fs: (layer, e[s], 0, nf + fs[s * nf + f])),
                      pl.BlockSpec((None, None, ft, d), lambda s, f, e, r, nb, fs: (layer, e[s], fs[s * nf + f], 0)),
                      pl.BlockSpec((None, None, 1, d), lambda s, f, e, r, nb, fs: (layer, e[s], 0, 0))],
            out_specs=pl.BlockSpec(memory_space=pl.ANY),
            scratch_shapes=[pltpu.VMEM((MOE_SUPER, d), _bf16), pltpu.VMEM((MOE_SUPER, d), _f32),
                            pltpu.VMEM((MOE_STAGE, MOE_BLK, d), _f32),
                            pltpu.VMEM((d, ft), _bf16), pltpu.VMEM((d, ft), _bf16), pltpu.VMEM((ft, d), _bf16),
                            pltpu.SMEM((1,), jnp.int32),
                            pltpu.SemaphoreType.DMA((MOE_STAGE,)),
                            pltpu.SemaphoreType.DMA((max_blk,))]),
        out_shape=jax.ShapeDtypeStruct((p, d), _f32),
        input_output_aliases={4: 0},
        compiler_params=_params("arbitrary", "arbitrary"),
        name="moe_experts",
    )(*tables, xs, w_gu, w_gu, b_gu3, b_gu3, w_d, b_d3)


def _combine_kernel(dest_ref, y_ref, gate_ref, res_ref, g_ref, be_ref, o_ref, buf_ref, sem, *, tm, alpha):
    base = pl.program_id(0) * tm * TOP_K

    def issue(r, carry):
        for k in range(TOP_K):
            _row_copy(y_ref, dest_ref[base + r * TOP_K + k], buf_ref.at[k], r, sem).start(priority=k % DMA_PRIORITIES)
        return carry

    def drain(r, carry):
        for k in range(TOP_K):
            _row_copy(y_ref, dest_ref[base + r * TOP_K + k], buf_ref.at[k], r, sem).wait()
        return carry

    lax.fori_loop(0, tm, issue, 0)
    lax.fori_loop(0, tm, drain, 0)
    gates = gate_ref[...]
    mix = gates[:, 0:1] * buf_ref[0]
    for k in range(1, TOP_K):
        mix = mix + gates[:, k:k + 1] * buf_ref[k]
    o_ref[...] = _ln(alpha * res_ref[...] + mix, g_ref[...], be_ref[...])


def _combine(y, dest, gates, res, g, be, *, alpha):
    n, d = res.shape
    tm = _row_tile(n, 128)
    return pl.pallas_call(
        functools.partial(_combine_kernel, tm=tm, alpha=alpha),
        grid_spec=pltpu.PrefetchScalarGridSpec(
            num_scalar_prefetch=1,
            grid=(n // tm,),
            in_specs=[pl.BlockSpec(memory_space=pl.ANY),
                      pl.BlockSpec((tm, LANE), lambda i, dest: (i, 0)),
                      pl.BlockSpec((tm, d), lambda i, dest: (i, 0)),
                      pl.BlockSpec((1, d), lambda i, dest: (0, 0)),
                      pl.BlockSpec((1, d), lambda i, dest: (0, 0))],
            out_specs=pl.BlockSpec((tm, d), lambda i, dest: (i, 0)),
            scratch_shapes=[pltpu.VMEM((TOP_K, tm, d), _f32), pltpu.SemaphoreType.DMA(())]),
        out_shape=jax.ShapeDtypeStruct((n, d), _f32),
        compiler_params=_params("arbitrary"),
        name="moe_combine",
    )(dest, y, gates, res, g, be)


def _moe_tables(idx, n_e, n_sb, nf):
    flat_e = idx[:, :TOP_K].reshape(-1)
    onehot = (flat_e[:, None] == jnp.arange(n_e, dtype=jnp.int32)[None, :]).astype(jnp.int32)
    cum = jnp.cumsum(onehot, axis=0)
    counts = cum[-1]
    rank = jnp.sum(onehot * cum, axis=1) - 1
    padded = (counts + MOE_BLK - 1) // MOE_BLK * MOE_BLK
    pad_start = jnp.cumsum(padded) - padded
    dest = (pad_start[flat_e] + rank).astype(jnp.int32)

    n_super = (padded + MOE_SUPER - 1) // MOE_SUPER
    sb_end = jnp.cumsum(n_super)
    total = sb_end[-1]
    s = jnp.arange(n_sb, dtype=jnp.int32)
    valid = s < total
    s_c = jnp.minimum(s, total - 1)
    e = jnp.minimum(jnp.searchsorted(sb_end, s_c, side='right'), n_e - 1).astype(jnp.int32)
    j = s_c - (sb_end[e] - n_super[e])
    row0 = (pad_start[e] + j * MOE_SUPER).astype(jnp.int32)
    nblk = jnp.clip((padded[e] - j * MOE_SUPER) // MOE_BLK, 0, MOE_SUPER // MOE_BLK)
    nblk = jnp.where(valid, nblk, 0).astype(jnp.int32)
    f_idx = jnp.where(valid[:, None], jnp.arange(nf, dtype=jnp.int32)[None, :], nf - 1).reshape(-1).astype(jnp.int32)
    return dest, (e, row0, nblk, f_idx)


def kernel(x_prompt, x_sample, mem_prompt, state_conv, state_hgrn, cache_swa_k, cache_swa_v, cache_mem_k, cache_mem_v, ln_g, ln_b, conv_w_in, conv_b_in, conv_w_dw, conv_b_dw, conv_ln_g, conv_ln_b, conv_w_out, conv_b_out, hgrn_w_in, hgrn_lb, hgrn_norm_g, hgrn_w_out, swa_w_qkv, swa_b_qkv, swa_sinks, swa_w_out, swa_b_out, mem_w_q, mem_w_kv, mem_w_out, moe_w_router, moe_b_router, moe_w_gate_up, moe_b_gate_up, moe_w_down, moe_b_down):
    bp, tp, d = x_prompt.shape
    bs, ts, _ = x_sample.shape
    assert bp == 1, "the prompt group is one stream"
    depth = ln_g.shape[0]
    n = tp + bs * ts
    n_e = moe_w_router.shape[-1]
    n_kv = cache_swa_k.shape[3]
    keep = cache_swa_k.shape[2]
    mem_heads = cache_mem_k.shape[3]
    m_tok = mem_prompt.shape[1]
    kw = conv_w_dw.shape[1]
    conv_state = kw - 1
    nh = d // HGRN_HEAD_DIM
    alpha = (2 * depth) ** 0.25
    grp = dict(t_prompt=tp, n_batch_s=bs, t_s=ts)
    assert ts >= conv_state and keep == WINDOW_CHUNKS * CHUNK and n_e <= LANE

    def row(v):
        return v.reshape(1, -1).astype(_f32)

    zeros_d = jnp.zeros((1, d), _f32)
    lb_p = jax.nn.softmax(hgrn_lb.astype(_f32), axis=0)
    lower_bounds = jnp.cumsum(lb_p, axis=0) - lb_p[0]

    n_assign = n * TOP_K
    p_rows = ((n_assign + n_e * (MOE_BLK - 1)) // MOE_BLK + 1) * MOE_BLK
    n_sb = (n_assign + n_e * (MOE_BLK - 1)) // MOE_SUPER + n_e
    nf = moe_w_down.shape[2] // min(MOE_FT, moe_w_down.shape[2])
    xs_buf = jnp.zeros((p_rows, d), _f32)

    x = jnp.concatenate([x_prompt.reshape(tp, d), x_sample.reshape(bs * ts, d)], axis=0)
    p_conv, s_conv, p_hgrn, s_hgrn, p_k, p_v, s_k, s_v, p_mk, p_mv = ([] for _ in range(10))

    for i in range(depth):
        kind, slot = i % 3, i // 3
        if kind == 0:
            u = _mm_glu(x, conv_w_in[slot].astype(_bf16), row(conv_b_in[slot]))
            hist = jnp.pad(state_conv[slot], ((0, 0), (HIST_ROWS - conv_state, 0), (0, 0)))
            c = _conv(u, hist, conv_w_dw[slot], row(conv_b_dw[slot]), **grp)
            p_conv.append(u[tp - conv_state:tp][None])
            s_conv.append(u[tp:].reshape(bs, ts, d)[:, ts - conv_state:])
            x = _mm_res_ln(c, conv_w_out[slot].astype(_bf16), row(conv_b_out[slot]), x,
                           row(ln_g[i, 0]), row(ln_b[i, 0]), alpha=alpha,
                           pre_norm=(row(conv_ln_g[slot]), row(conv_ln_b[slot])))
        elif kind == 1:
            proj = _mm(x, hgrn_w_in[slot].astype(_bf16), jnp.zeros((1, 4 * d), _f32))
            lbv, ngv = row(lower_bounds[i]), row(hgrn_norm_g[slot])
            s0_p = jnp.zeros((1, nh, HGRN_HEAD_DIM, HGRN_HEAD_DIM), _f32)
            s0_s = jnp.swapaxes(state_hgrn[slot].astype(_f32), -1, -2)
            o, sf_p = _hgrn(proj, lbv, ngv, s0_p, None, row_off=0, n_batch=1, t_len=tp, name="hgrn_prompt")
            o, sf_s = _hgrn(proj, lbv, ngv, s0_s, o, row_off=tp, n_batch=bs, t_len=ts, name="hgrn_sample")
            p_hgrn.append(jnp.swapaxes(sf_p, -1, -2))
            s_hgrn.append(jnp.swapaxes(sf_s, -1, -2))
            x = _mm_res_ln(o, hgrn_w_out[slot].astype(_bf16), zeros_d, x,
                           row(ln_g[i, 0]), row(ln_b[i, 0]), alpha=alpha)
        else:
            qkv = _mm(x, swa_w_qkv[slot].astype(_bf16), row(swa_b_qkv[slot]), tn=swa_w_qkv.shape[-1])
            kvw = n_kv * SWA_HEAD_DIM
            qw = qkv.shape[1] - 2 * kvw
            ck = cache_swa_k[slot].reshape(bs, keep, kvw)
            cv = cache_swa_v[slot].reshape(bs, keep, kvw)
            o = _swa(qkv, swa_sinks[slot].astype(_f32), ck, cv, n_kv=n_kv, **grp)
            k_new, v_new = qkv[:, qw:qw + kvw], qkv[:, qw + kvw:]
            p_k.append(k_new[tp - keep:tp].reshape(1, keep, n_kv, SWA_HEAD_DIM))
            p_v.append(v_new[tp - keep:tp].reshape(1, keep, n_kv, SWA_HEAD_DIM))
            s_k.append(jnp.concatenate([ck, k_new[tp:].reshape(bs, ts, kvw)], axis=1)[:, -keep:]
                       .reshape(bs, keep, n_kv, SWA_HEAD_DIM))
            s_v.append(jnp.concatenate([cv, v_new[tp:].reshape(bs, ts, kvw)], axis=1)[:, -keep:]
                       .reshape(bs, keep, n_kv, SWA_HEAD_DIM))
            x = _mm_res_ln(o, swa_w_out[slot].astype(_bf16), row(swa_b_out[slot]), x,
                           row(ln_g[i, 0]), row(ln_b[i, 0]), alpha=alpha)

        kv_p = _mm(mem_prompt.reshape(m_tok, d), mem_w_kv[i].astype(_bf16), jnp.zeros((1, 2 * d), _f32))
        p_mk.append(kv_p[:, :d].reshape(1, m_tok, mem_heads, d // mem_heads))
        p_mv.append(kv_p[:, d:].reshape(1, m_tok, mem_heads, d // mem_heads))
        q = _mm(x, mem_w_q[i].astype(_bf16), zeros_d)
        o = _mem_attend(q, kv_p, cache_mem_k, cache_mem_v, i, n_heads=mem_heads, **grp)
        x = _mm_res_ln(o, mem_w_out[i].astype(_bf16), zeros_d, x, row(ln_g[i, 1]), row(ln_b[i, 1]), alpha=alpha)

        w_r = jnp.pad(moe_w_router[i].astype(_f32), ((0, 0), (0, LANE - n_e)))
        b_r = jnp.pad(moe_b_router[i].astype(_f32), (0, LANE - n_e), constant_values=NEG).reshape(1, LANE)
        gates, idx = _router(x, w_r, b_r)
        dest, tables = _moe_tables(idx, n_e, n_sb, nf)
        xs_buf = _scatter_rows(x, dest, xs_buf)
        xs_buf = _experts(xs_buf, tables, moe_w_gate_up, moe_b_gate_up, moe_w_down, moe_b_down, i)
        x = _combine(xs_buf, dest, gates, x, row(ln_g[i, 2]), row(ln_b[i, 2]), alpha=alpha)

    y_prompt = x[:tp].reshape(1, tp, d)
    y_sample = x[tp:].reshape(bs, ts, d)
    return (y_prompt, y_sample, jnp.stack(p_conv), jnp.stack(p_hgrn), jnp.stack(p_k), jnp.stack(p_v),
            jnp.stack(p_mk), jnp.stack(p_mv), jnp.stack(s_conv), jnp.stack(s_hgrn), jnp.stack(s_k), jnp.stack(s_v))
```

```python
import functools

import jax
import jax.numpy as jnp
from jax import lax
from jax.experimental import pallas as pl
from jax.experimental.pallas import tpu as pltpu

PAST_LEN = 1024
CHUNK = 64
WINDOW_CHUNKS = 2
SWA_HEAD_DIM = 64
HGRN_HEAD_DIM = 128
TOP_K = 4
SWIGLU_LIMIT = 7.0
SWIGLU_ALPHA = 1.702
LN_EPS = 1e-5

LANE = 128
SUBLANE = 8
VMEM_LIMIT = 56 * 1024 * 1024
HIST_ROWS = 32
MOE_BLK = 256
MOE_SUPER = 1280
MOE_STAGE = 2
MOE_FT = 512
MOE_PAIR = 2
HGRN_SUB = 16
HGRN_HEADS_PER_STEP = 4
NEG = -1e30

_bf16 = jnp.bfloat16
_f32 = jnp.float32


def _params(*sem):
    return pltpu.CompilerParams(dimension_semantics=sem, vmem_limit_bytes=VMEM_LIMIT)


def _row_tile(n, cap=256):
    for t in (cap, 128, 64, 32, 16, 8):
        if t <= cap and n % t == 0:
            return t
    raise ValueError(f"row count {n} is not a multiple of {SUBLANE}")


def _tail_steps(n, n_own, tile):
    tail = n - n_own
    assert tail % tile == 0, "rows after a call's own must fill whole tiles"
    return tail // tile


def _ln(y, g, b):
    mu = jnp.mean(y, axis=-1, keepdims=True)
    d = y - mu
    var = jnp.mean(d * d, axis=-1, keepdims=True)
    return d * lax.rsqrt(var + LN_EPS) * g + b


def _dot(a, b):
    return jnp.dot(a, b, preferred_element_type=_f32)


def _dot_nt(a, b):
    return lax.dot_general(a, b, (((1,), (1,)), ((), ())), preferred_element_type=_f32)


def _mm_kernel(x_ref, w_ref, b_ref, o_ref):
    o_ref[...] = _dot(x_ref[...].astype(_bf16), w_ref[...]) + b_ref[...]


def _mm(x, w, b, *, tn=2048):
    m, k = x.shape
    n = w.shape[1]
    tn = min(tn, n)
    tm = _row_tile(m)
    return pl.pallas_call(
        _mm_kernel,
        grid=(n // tn, m // tm),
        in_specs=[pl.BlockSpec((tm, k), lambda j, i: (i, 0)),
                  pl.BlockSpec((k, tn), lambda j, i: (0, j)),
                  pl.BlockSpec((1, tn), lambda j, i: (0, j))],
        out_specs=pl.BlockSpec((tm, tn), lambda j, i: (i, j)),
        out_shape=jax.ShapeDtypeStruct((m, n), _f32),
        compiler_params=_params("arbitrary", "arbitrary"),
        name="mm_bias",
    )(x, w, b)


def _mm_glu_kernel(x_ref, wa_ref, wb_ref, ba_ref, bb_ref, o_ref):
    x = x_ref[...].astype(_bf16)
    a = _dot(x, wa_ref[...]) + ba_ref[...]
    g = _dot(x, wb_ref[...]) + bb_ref[...]
    o_ref[...] = a * jax.nn.sigmoid(g)


def _mm_glu(x, w, b, *, tn=1024):
    m, k = x.shape
    n = w.shape[1] // 2
    tn = min(tn, n)
    tm = _row_tile(m)
    nj = n // tn
    return pl.pallas_call(
        _mm_glu_kernel,
        grid=(nj, m // tm),
        in_specs=[pl.BlockSpec((tm, k), lambda j, i: (i, 0)),
                  pl.BlockSpec((k, tn), lambda j, i: (0, j)),
                  pl.BlockSpec((k, tn), lambda j, i: (0, j + nj)),
                  pl.BlockSpec((1, tn), lambda j, i: (0, j)),
                  pl.BlockSpec((1, tn), lambda j, i: (0, j + nj))],
        out_specs=pl.BlockSpec((tm, tn), lambda j, i: (i, j)),
        out_shape=jax.ShapeDtypeStruct((m, n), _f32),
        compiler_params=_params("arbitrary", "arbitrary"),
        name="mm_glu",
    )(x, w, w, b, b)


def _mm_res_ln_kernel(x_ref, w_ref, b_ref, res_ref, g_ref, be_ref, o_ref, *, alpha):
    h = _dot(x_ref[...].astype(_bf16), w_ref[...]) + b_ref[...]
    o_ref[...] = _ln(alpha * res_ref[...] + h, g_ref[...], be_ref[...])


def _mm_norm_res_ln_kernel(x_ref, ng_ref, nb_ref, w_ref, b_ref, res_ref, g_ref, be_ref, o_ref, *, alpha):
    c = _ln(x_ref[...], ng_ref[...], nb_ref[...])
    c = c * jax.nn.sigmoid(c)
    h = _dot(c.astype(_bf16), w_ref[...]) + b_ref[...]
    o_ref[...] = _ln(alpha * res_ref[...] + h, g_ref[...], be_ref[...])


def _mm_res_ln(x, w, b, res, g, be, *, alpha, pre_norm=None):
    m, k = x.shape
    n = w.shape[1]
    tm = _row_tile(m)
    row = lambda i: (i, 0)
    fix = lambda i: (0, 0)
    specs = [pl.BlockSpec((tm, k), row)]
    args = [x]
    if pre_norm is not None:
        specs += [pl.BlockSpec((1, k), fix), pl.BlockSpec((1, k), fix)]
        args += list(pre_norm)
        body = _mm_norm_res_ln_kernel
    else:
        body = _mm_res_ln_kernel
    specs += [pl.BlockSpec((k, n), fix), pl.BlockSpec((1, n), fix), pl.BlockSpec((tm, n), row),
              pl.BlockSpec((1, n), fix), pl.BlockSpec((1, n), fix)]
    args += [w, b, res, g, be]
    return pl.pallas_call(
        functools.partial(body, alpha=alpha),
        grid=(m // tm,),
        in_specs=specs,
        out_specs=pl.BlockSpec((tm, n), row),
        out_shape=jax.ShapeDtypeStruct((m, n), _f32),
        compiler_params=_params("arbitrary"),
        name="mm_res_ln",
    )(*args)


def _conv_kernel(hist_ref, u_ref, w_ref, b_ref, o_ref, buf_ref, shift_ref, *, n_real, tt, lb, kw):
    i = pl.program_id(0)

    def compute():
        hist = hist_ref[...]
        if n_real is not None:
            hist = jnp.where(i == 0, 0.0, hist)
        buf_ref[0:HIST_ROWS, :] = hist
        buf_ref[HIST_ROWS:, :] = u_ref[...]
        off = HIST_ROWS - (kw - 1)
        rb = min(tt, 64)
        for r in range(tt // rb):
            for c in range(lb // LANE):
                cs = slice(c * LANE, (c + 1) * LANE)
                acc = jnp.zeros((rb, LANE), _f32) + b_ref[:, cs]
                for s in range(SUBLANE):
                    taps = [p - off for p in range(s, off + kw, SUBLANE) if p >= off]
                    if not taps:
                        continue
                    rows = rb if s == 0 else rb + SUBLANE
                    v = None
                    for j in taps:
                        a0 = r * rb + off + j - s
                        term = w_ref[j:j + 1, cs] * buf_ref[a0:a0 + rows, cs]
                        v = term if v is None else v + term
                    if s == 0:
                        acc = acc + v
                    else:
                        shift_ref[s, 0:rows, :] = v
                        acc = acc + shift_ref[s, s:s + rb, :]
                o_ref[r * rb:(r + 1) * rb, cs] = acc

    if n_real is None:
        compute()
    else:
        pl.when(i < n_real)(compute)

        @pl.when(i >= n_real)
        def _():
            o_ref[...] = jnp.zeros(o_ref.shape, _f32)


def _conv(u, hist_s, w, b, *, t_prompt, n_batch_s, t_s):
    n, d = u.shape
    kw = w.shape[0]
    lb = min(512, d)
    tt = _row_tile(t_prompt)
    hb = tt // HIST_ROWS
    nt = t_prompt // tt
    last = nt - 1
    common = dict(out_shape=jax.ShapeDtypeStruct((n, d), _f32), compiler_params=_params("arbitrary", "arbitrary"))
    c = pl.pallas_call(
        functools.partial(_conv_kernel, n_real=nt, tt=tt, lb=lb, kw=kw),
        grid=(nt + _tail_steps(n, t_prompt, tt), d // lb),
        in_specs=[pl.BlockSpec((HIST_ROWS, lb), lambda i, j: (jnp.maximum(jnp.minimum(i, last) * hb - 1, 0), j)),
                  pl.BlockSpec((tt, lb), lambda i, j: (jnp.minimum(i, last), j)),
                  pl.BlockSpec((kw, lb), lambda i, j: (0, j)),
                  pl.BlockSpec((1, lb), lambda i, j: (0, j))],
        out_specs=pl.BlockSpec((tt, lb), lambda i, j: (i, j)),
        scratch_shapes=[pltpu.VMEM((HIST_ROWS + tt, lb), _f32),
                        pltpu.VMEM((SUBLANE, min(tt, 64) + SUBLANE, LANE), _f32)],
        name="conv_prompt", **common,
    )(u, u, w, b)
    ob = t_prompt // t_s
    return pl.pallas_call(
        lambda full_ref, *refs: _conv_kernel(*refs, n_real=None, tt=t_s, lb=lb, kw=kw),
        grid=(n_batch_s, d // lb),
        in_specs=[pl.BlockSpec(memory_space=pl.ANY),
                  pl.BlockSpec((None, HIST_ROWS, lb), lambda i, j: (i, 0, j)),
                  pl.BlockSpec((t_s, lb), lambda i, j: (ob + i, j)),
                  pl.BlockSpec((kw, lb), lambda i, j: (0, j)),
                  pl.BlockSpec((1, lb), lambda i, j: (0, j))],
        out_specs=pl.BlockSpec((t_s, lb), lambda i, j: (ob + i, j)),
        scratch_shapes=[pltpu.VMEM((HIST_ROWS + t_s, lb), _f32),
                        pltpu.VMEM((SUBLANE, min(t_s, 64) + SUBLANE, LANE), _f32)],
        input_output_aliases={0: 0},
        name="conv_sample", **common,
    )(c, hist_s, u, w, b)


def _hgrn_kernel(q_ref, f_ref, i_ref, og_ref, lb_ref, ng_ref, s0_ref, o_ref, sf_ref, st_ref, *,
                 tt, blk, heads, n_real):
    t = pl.program_id(2)
    hd = HGRN_HEAD_DIM

    @pl.when(t == 0)
    def _():
        st_ref[...] = s0_ref[...]

    rr = lax.broadcasted_iota(jnp.int32, (blk, blk), 0)
    cc = lax.broadcasted_iota(jnp.int32, (blk, blk), 1)
    tril = (cc <= rr).astype(_f32)
    sub_row = lax.broadcasted_iota(jnp.int32, (HGRN_SUB, LANE), 0)

    def head_chunk(base, hh):
        rows = pl.ds(base, blk)
        ls = slice(hh * hd, (hh + 1) * hd)
        lbv = lb_ref[:, ls]
        q = q_ref[rows, ls]
        qh = q * jax.nn.sigmoid(q)
        forget = lbv + (1.0 - lbv) * jax.nn.sigmoid(f_ref[rows, ls])
        kh = 1.0 - forget
        v = i_ref[rows, ls]
        b = jnp.dot(tril, jnp.log(forget), precision=lax.Precision.HIGHEST, preferred_element_type=_f32)
        st = st_ref[hh]
        inter = _dot_nt((qh * jnp.exp(b)).astype(_bf16), st.astype(_bf16))
        b_last = b[blk - 1:blk, :]
        kf_end = (kh * jnp.exp(b_last - b)).astype(_bf16)
        st_ref[hh] = st * jnp.exp(b_last) + lax.dot_general(
            v.astype(_bf16), kf_end, (((0,), (0,)), ((), ())), preferred_element_type=_f32)
        vb = v.astype(_bf16)
        for blk_i in range(blk // HGRN_SUB):
            r0 = blk_i * HGRN_SUB
            b_i = b[r0:r0 + HGRN_SUB, :]
            q_i = qh[r0:r0 + HGRN_SUB, :]
            acc = inter[r0:r0 + HGRN_SUB, :]
            for s in range(HGRN_SUB):
                keep = sub_row >= s
                decay = jnp.exp(jnp.where(keep, b_i - b[r0 + s:r0 + s + 1, :], NEG))
                col = jnp.sum(q_i * decay * kh[r0 + s:r0 + s + 1, :], axis=-1, keepdims=True)
                acc = acc + col * v[r0 + s:r0 + s + 1, :]
            if r0 > 0:
                anchor = b[r0 - 1:r0, :]
                qf = (q_i * jnp.exp(b_i - anchor)).astype(_bf16)
                kf = (kh[0:r0, :] * jnp.exp(anchor - b[0:r0, :])).astype(_bf16)
                acc = acc + _dot(_dot_nt(qf, kf).astype(_bf16), vb[0:r0, :])
            o = acc * lax.rsqrt(jnp.mean(acc * acc, axis=-1, keepdims=True) + LN_EPS)
            o_rows = pl.ds(base + r0, HGRN_SUB)
            o_ref[o_rows, ls] = o * ng_ref[:, ls] * jax.nn.sigmoid(og_ref[o_rows, ls])

    def chunk(c, carry):
        base = pl.multiple_of(c * blk, blk)
        for hh in range(heads):
            head_chunk(base, hh)
        return carry

    @pl.when(t < n_real)
    def _():
        lax.fori_loop(0, tt // blk, chunk, 0)

    @pl.when(t >= n_real)
    def _():
        o_ref[...] = jnp.zeros(o_ref.shape, _f32)

    @pl.when(t == n_real - 1)
    def _():
        sf_ref[...] = st_ref[...]


def _hgrn(proj, lb, ng, s0_t, o_prev, *, row_off, n_batch, t_len, name):
    n, d4 = proj.shape
    d = d4 // 4
    hd = HGRN_HEAD_DIM
    nh = d // hd
    heads = HGRN_HEADS_PER_STEP if nh % HGRN_HEADS_PER_STEP == 0 else 1
    n_groups = nh // heads
    w = heads * hd
    blk = min(t_len, CHUNK)
    tt = _row_tile(t_len)
    nt = t_len // tt
    last = nt - 1
    ob = row_off // tt
    tail = _tail_steps(n, row_off + n_batch * t_len, tt) if o_prev is None else 0
    row = lambda k: (lambda b, h, t: (ob + b * nt + jnp.minimum(t, last), k * n_groups + h))
    head = lambda b, h, t: (0, h)
    in_specs = [pl.BlockSpec((tt, w), row(0)), pl.BlockSpec((tt, w), row(1)),
                pl.BlockSpec((tt, w), row(2)), pl.BlockSpec((tt, w), row(3)),
                pl.BlockSpec((1, w), head), pl.BlockSpec((1, w), head),
                pl.BlockSpec((None, heads, hd, hd), lambda b, h, t: (b, h, 0, 0))]
    args = [proj, proj, proj, proj, lb, ng, s0_t]
    body = functools.partial(_hgrn_kernel, tt=tt, blk=blk, heads=heads, n_real=nt)
    aliases = {}
    if o_prev is not None:
        in_specs = [pl.BlockSpec(memory_space=pl.ANY)] + in_specs
        args = [o_prev] + args
        inner = body
        body = lambda full_ref, *refs: inner(*refs)
        aliases = {0: 0}
    return pl.pallas_call(
        body,
        grid=(n_batch, n_groups, nt + tail),
        in_specs=in_specs,
        out_specs=[pl.BlockSpec((tt, w), lambda b, h, t: (ob + b * nt + t, h)),
                   pl.BlockSpec((None, heads, hd, hd), lambda b, h, t: (b, h, 0, 0))],
        out_shape=[jax.ShapeDtypeStruct((n, d), _f32),
                   jax.ShapeDtypeStruct((n_batch, nh, hd, hd), _f32)],
        scratch_shapes=[pltpu.VMEM((heads, hd, hd), _f32)],
        input_output_aliases=aliases,
        compiler_params=_params("arbitrary", "arbitrary", "arbitrary"),
        name=name,
    )(*args)


def _swa_kernel(sink_ref, q_ref, kc_ref, vc_ref, kp_ref, vp_ref, o_ref, *, tq, win, pos0, n_kv, group, n_real):
    dh = SWA_HEAD_DIM
    i = pl.program_id(1)
    n_heads = n_kv * group

    def compute():
        qpos0 = pos0 + i * tq

        def bias_terms(nk, kpos0):
            qp = qpos0 + lax.broadcasted_iota(jnp.int32, (tq, nk), 0)
            kp = kpos0 + lax.broadcasted_iota(jnp.int32, (tq, nk), 1)
            qc = qp // CHUNK
            kc = jnp.maximum(kp, 0) // CHUNK
            vis = (kp >= 0) & (kc <= qc) & (qc - kc <= WINDOW_CHUNKS)
            return vis, jnp.abs(qp - kp).astype(_f32)

        vis_p, dist_p = bias_terms(win, qpos0 - win)
        vis_c, dist_c = bias_terms(tq, qpos0)
        scale = dh ** -0.5
        for kv in range(n_kv):
            ks = slice(kv * dh, (kv + 1) * dh)
            kp = kp_ref[:, ks].astype(_bf16)
            kc = kc_ref[:, ks].astype(_bf16)
            vp = vp_ref[:, ks].astype(_bf16)
            vc = vc_ref[:, ks].astype(_bf16)
            heads = [kv * group + g for g in range(group)]
            qs = jnp.concatenate([q_ref[:, h * dh:(h + 1) * dh] for h in heads], axis=0).astype(_bf16)
            raw_p = _dot_nt(qs, kp)
            raw_c = _dot_nt(qs, kc)
            w_p, w_c = [], []
            for g, h in enumerate(heads):
                slope = 2.0 ** (-8.0 * (h + 1) / n_heads)
                rs = slice(g * tq, (g + 1) * tq)
                s_p = jnp.where(vis_p, raw_p[rs] * scale - slope * dist_p, NEG)
                s_c = jnp.where(vis_c, raw_c[rs] * scale - slope * dist_c, NEG)
                sink = sink_ref[h]
                m = jnp.maximum(jnp.maximum(jnp.max(s_p, axis=-1, keepdims=True),
                                            jnp.max(s_c, axis=-1, keepdims=True)), sink)
                e_p = jnp.exp(s_p - m)
                e_c = jnp.exp(s_c - m)
                den = (jnp.sum(e_p, axis=-1, keepdims=True) + jnp.sum(e_c, axis=-1, keepdims=True)
                       + jnp.exp(sink - m))
                inv = 1.0 / den
                w_p.append((e_p * inv).astype(_bf16))
                w_c.append((e_c * inv).astype(_bf16))
            o = _dot(jnp.concatenate(w_p, axis=0), vp) + _dot(jnp.concatenate(w_c, axis=0), vc)
            for g, h in enumerate(heads):
                o_ref[:, h * dh:(h + 1) * dh] = o[g * tq:(g + 1) * tq]

    if n_real is None:
        compute()
    else:
        pl.when(i < n_real)(compute)

        @pl.when(i >= n_real)
        def _():
            o_ref[...] = jnp.zeros(o_ref.shape, _f32)


def _swa(qkv, sinks, cache_k, cache_v, *, t_prompt, n_batch_s, t_s, n_kv):
    n, width = qkv.shape
    dh = SWA_HEAD_DIM
    kvw = n_kv * dh
    qw = width - 2 * kvw
    group = qw // kvw
    win = WINDOW_CHUNKS * CHUNK
    kcol, vcol = qw // kvw, qw // kvw + 1
    common = dict(out_shape=jax.ShapeDtypeStruct((n, qw), _f32), compiler_params=_params("arbitrary", "arbitrary"))
    smem = pl.BlockSpec(memory_space=pltpu.SMEM)
    tq = win
    nt = t_prompt // tq
    last = nt - 1
    cur = lambda c: (lambda b, i: (jnp.minimum(i, last), c))
    prev = lambda c: (lambda b, i: (jnp.maximum(jnp.minimum(i, last) - 1, 0), c))
    o = pl.pallas_call(
        functools.partial(_swa_kernel, tq=tq, win=win, pos0=0, n_kv=n_kv, group=group, n_real=nt),
        grid=(1, nt + _tail_steps(n, t_prompt, tq)),
        in_specs=[smem,
                  pl.BlockSpec((tq, qw), cur(0)),
                  pl.BlockSpec((tq, kvw), cur(kcol)),
                  pl.BlockSpec((tq, kvw), cur(vcol)),
                  pl.BlockSpec((win, kvw), prev(kcol)),
                  pl.BlockSpec((win, kvw), prev(vcol))],
        out_specs=pl.BlockSpec((tq, qw), lambda b, i: (i, 0)),
        name="swa_prompt", **common,
    )(sinks, qkv, qkv, qkv, qkv, qkv)
    ob = t_prompt // t_s
    inner = functools.partial(_swa_kernel, tq=t_s, win=win, pos0=PAST_LEN, n_kv=n_kv, group=group, n_real=None)
    return pl.pallas_call(
        lambda full_ref, *refs: inner(*refs),
        grid=(n_batch_s, 1),
        in_specs=[pl.BlockSpec(memory_space=pl.ANY), smem,
                  pl.BlockSpec((t_s, qw), lambda b, i: (ob + b, 0)),
                  pl.BlockSpec((t_s, kvw), lambda b, i: (ob + b, kcol)),
                  pl.BlockSpec((t_s, kvw), lambda b, i: (ob + b, vcol)),
                  pl.BlockSpec((None, win, kvw), lambda b, i: (b, 0, 0)),
                  pl.BlockSpec((None, win, kvw), lambda b, i: (b, 0, 0))],
        out_specs=pl.BlockSpec((t_s, qw), lambda b, i: (ob + b, 0)),
        input_output_aliases={0: 0},
        name="swa_sample", **common,
    )(o, sinks, qkv, qkv, qkv, cache_k, cache_v)


def _mem_kernel(q_ref, k_ref, v_ref, o_ref, *, scale, n_real):
    def compute():
        s = _dot_nt(q_ref[...].astype(_bf16), k_ref[...].astype(_bf16)) * scale
        e = jnp.exp(s - jnp.max(s, axis=-1, keepdims=True))
        w = e * (1.0 / jnp.sum(e, axis=-1, keepdims=True))
        o_ref[...] = _dot(w.astype(_bf16), v_ref[...].astype(_bf16))

    if n_real is None:
        compute()
    else:
        i = pl.program_id(1)
        pl.when(i < n_real)(compute)

        @pl.when(i >= n_real)
        def _():
            o_ref[...] = jnp.zeros(o_ref.shape, _f32)


def _mem_attend(q, kv_p, k_s, v_s, layer, *, t_prompt, n_batch_s, t_s, n_heads):
    n, d = q.shape
    dh = d // n_heads
    m = kv_p.shape[0]
    common = dict(out_shape=jax.ShapeDtypeStruct((n, d), _f32), compiler_params=_params("arbitrary", "arbitrary"))
    tq = _row_tile(t_prompt)
    nt = t_prompt // tq
    last = nt - 1
    o = pl.pallas_call(
        functools.partial(_mem_kernel, scale=dh ** -0.5, n_real=nt),
        grid=(n_heads, nt + _tail_steps(n, t_prompt, tq)),
        in_specs=[pl.BlockSpec((tq, dh), lambda h, i: (jnp.minimum(i, last), h)),
                  pl.BlockSpec((m, dh), lambda h, i: (0, h)),
                  pl.BlockSpec((m, dh), lambda h, i: (0, n_heads + h))],
        out_specs=pl.BlockSpec((tq, dh), lambda h, i: (i, h)),
        name="mem_prompt", **common,
    )(q, kv_p, kv_p)
    ob = t_prompt // t_s

    def sample_body(full_ref, q_ref, k_ref, v_ref, o_ref):
        del full_ref
        for h in range(n_heads):
            hs = slice(h * dh, (h + 1) * dh)
            s = _dot_nt(q_ref[:, hs].astype(_bf16), k_ref[:, h, :].astype(_bf16)) * dh ** -0.5
            e = jnp.exp(s - jnp.max(s, axis=-1, keepdims=True))
            w = e * (1.0 / jnp.sum(e, axis=-1, keepdims=True))
            o_ref[:, hs] = _dot(w.astype(_bf16), v_ref[:, h, :].astype(_bf16))

    cache_spec = pl.BlockSpec((None, None, m, n_heads, dh), lambda b: (layer, b, 0, 0, 0))
    return pl.pallas_call(
        sample_body,
        grid=(n_batch_s,),
        in_specs=[pl.BlockSpec(memory_space=pl.ANY),
                  pl.BlockSpec((t_s, d), lambda b: (ob + b, 0)),
                  cache_spec, cache_spec],
        out_specs=pl.BlockSpec((t_s, d), lambda b: (ob + b, 0)),
        out_shape=jax.ShapeDtypeStruct((n, d), _f32),
        input_output_aliases={0: 0},
        compiler_params=_params("arbitrary"),
        name="mem_sample",
    )(o, q, k_s, v_s)


def _router_kernel(x_ref, w_ref, b_ref, gate_ref, idx_ref):
    logits = jnp.dot(x_ref[...], w_ref[...], precision=lax.Precision.HIGHEST,
                     preferred_element_type=_f32) + b_ref[...]
    lane = lax.broadcasted_iota(jnp.int32, logits.shape, 1)
    lane_f = lane.astype(_f32)
    vals, ids = [], []
    for _ in range(TOP_K):
        m = jnp.max(logits, axis=-1, keepdims=True)
        am = jnp.min(jnp.where(logits == m, lane_f, float(LANE)), axis=-1, keepdims=True)
        vals.append(m)
        ids.append(am)
        logits = jnp.where(lane_f == am, -jnp.inf, logits)
    es = [jnp.exp(v - vals[0]) for v in vals]
    inv = 1.0 / sum(es)
    gates = jnp.zeros(logits.shape, _f32)
    idx = jnp.zeros(logits.shape, _f32)
    for k in range(TOP_K):
        gates = jnp.where(lane == k, es[k] * inv, gates)
        idx = jnp.where(lane == k, ids[k], idx)
    gate_ref[...] = gates
    idx_ref[...] = idx.astype(jnp.int32)


def _router(x, w_pad, b_pad):
    n, d = x.shape
    tm = _row_tile(n)
    return pl.pallas_call(
        _router_kernel,
        grid=(n // tm,),
        in_specs=[pl.BlockSpec((tm, d), lambda i: (i, 0)),
                  pl.BlockSpec((d, LANE), lambda i: (0, 0)),
                  pl.BlockSpec((1, LANE), lambda i: (0, 0))],
        out_specs=[pl.BlockSpec((tm, LANE), lambda i: (i, 0)), pl.BlockSpec((tm, LANE), lambda i: (i, 0))],
        out_shape=[jax.ShapeDtypeStruct((n, LANE), _f32), jax.ShapeDtypeStruct((n, LANE), jnp.int32)],
        compiler_params=_params("arbitrary"),
        name="router",
    )(x, w_pad, b_pad)


def _row_copy(src, s, dst, d, sem):
    return pltpu.make_async_copy(src.at[pl.ds(s, 1), :], dst.at[pl.ds(d, 1), :], sem)


def _scatter_kernel(dest_ref, x_ref, xs_in_ref, xs_ref, sem, *, tm):
    del xs_in_ref
    base = pl.program_id(0) * tm * TOP_K

    def issue(r, carry):
        for k in range(TOP_K):
            _row_copy(x_ref, r, xs_ref, dest_ref[base + r * TOP_K + k], sem).start()
        return carry

    def drain(r, carry):
        for k in range(TOP_K):
            _row_copy(x_ref, r, xs_ref, dest_ref[base + r * TOP_K + k], sem).wait()
        return carry

    lax.fori_loop(0, tm, issue, 0)
    lax.fori_loop(0, tm, drain, 0)


def _scatter_rows(x, dest, xs_buf):
    n, d = x.shape
    tm = _row_tile(n, 128)
    return pl.pallas_call(
        functools.partial(_scatter_kernel, tm=tm),
        grid_spec=pltpu.PrefetchScalarGridSpec(
            num_scalar_prefetch=1,
            grid=(n // tm,),
            in_specs=[pl.BlockSpec((tm, d), lambda i, dest: (i, 0)),
                      pl.BlockSpec(memory_space=pl.ANY)],
            out_specs=pl.BlockSpec(memory_space=pl.ANY),
            scratch_shapes=[pltpu.SemaphoreType.DMA(())]),
        out_shape=jax.ShapeDtypeStruct(xs_buf.shape, _f32),
        input_output_aliases={2: 0},
        compiler_params=_params("arbitrary"),
        name="moe_scatter",
    )(dest, x, xs_buf)


def _expert_kernel(sb_e_ref, sb_row_ref, sb_nblk_ref, sb_f_ref, xs_ref, wg_ref, wl_ref, bg_ref, bl_ref,
                   wd_ref, bd_ref, out_ref, xb_ref, acc_ref, stage_ref, wgb_ref, wlb_ref, wdb_ref,
                   pend_ref, sem_in, sem_out, *, nf):
    del sb_e_ref, sb_f_ref
    s = pl.program_id(0)
    f = pl.program_id(1)
    nblk = sb_nblk_ref[s]
    row0 = pl.multiple_of(sb_row_ref[s], MOE_BLK)
    max_blk = xb_ref.shape[0] // MOE_BLK
    n_stage = stage_ref.shape[0]

    def rows_of(j):
        return pl.ds(j * MOE_BLK, MOE_BLK)

    def in_copy(j):
        return pltpu.make_async_copy(xs_ref.at[pl.ds(row0 + j * MOE_BLK, MOE_BLK), :],
                                     stage_ref.at[j % n_stage], sem_in.at[j % n_stage])

    def out_copy(j):
        return pltpu.make_async_copy(acc_ref.at[rows_of(j), :],
                                     out_ref.at[pl.ds(row0 + j * MOE_BLK, MOE_BLK), :], sem_out.at[j])

    def drain_out():
        pending = pend_ref[0]
        for j in range(max_blk):
            @pl.when(j < pending)
            def _():
                out_copy(j).wait()
        pend_ref[0] = 0

    @pl.when((s == 0) & (f == 0))
    def _():
        pend_ref[0] = 0

    @pl.when((f == 0) & (nblk > 0))
    def _():
        for j in range(min(n_stage, max_blk)):
            @pl.when(j < nblk)
            def _():
                in_copy(j).start()
        for j in range(max_blk):
            @pl.when(j < nblk)
            def _():
                in_copy(j).wait()
                xb_ref[rows_of(j), :] = stage_ref[j % n_stage].astype(_bf16)
                if j + n_stage < max_blk:
                    @pl.when(j + n_stage < nblk)
                    def _():
                        in_copy(j + n_stage).start()
        drain_out()

    @pl.when(nblk > 0)
    def _():
        wgb_ref[...] = wg_ref[...].astype(_bf16)
        wlb_ref[...] = wl_ref[...].astype(_bf16)
        wdb_ref[...] = wd_ref[...].astype(_bf16)

        def rows_mlp(rows, first):
            x = xb_ref[rows, :]
            gate = jnp.minimum(_dot(x, wgb_ref[...]) + bg_ref[...], SWIGLU_LIMIT)
            lin = jnp.clip(_dot(x, wlb_ref[...]) + bl_ref[...], -SWIGLU_LIMIT, SWIGLU_LIMIT)
            act = gate * jax.nn.sigmoid(SWIGLU_ALPHA * gate) * (lin + 1.0)
            part = _dot(act.astype(_bf16), wdb_ref[...])
            if first:
                acc_ref[rows, :] = part + bd_ref[...]
            else:
                acc_ref[rows, :] += part

        def sweep(first):
            group = MOE_PAIR * MOE_BLK
            n_group = nblk // MOE_PAIR

            def trip(j, carry):
                rows_mlp(pl.ds(pl.multiple_of(j * group, group), group), first)
                return carry

            lax.fori_loop(0, n_group, trip, 0)
            for extra in range(MOE_PAIR - 1):
                @pl.when(n_group * MOE_PAIR + extra < nblk)
                def _():
                    start = pl.multiple_of((n_group * MOE_PAIR + extra) * MOE_BLK, MOE_BLK)
                    rows_mlp(pl.ds(start, MOE_BLK), first)

        @pl.when(f == 0)
        def _():
            sweep(True)

        @pl.when(f > 0)
        def _():
            sweep(False)

    @pl.when((f == nf - 1) & (nblk > 0))
    def _():
        for j in range(max_blk):
            @pl.when(j < nblk)
            def _():
                out_copy(j).start()
        pend_ref[0] = nblk

    @pl.when((s == pl.num_programs(0) - 1) & (f == nf - 1))
    def _():
        drain_out()


def _experts(xs, tables, w_gu, b_gu, w_d, b_d, layer):
    p, d = xs.shape
    n_e, _, ff2 = w_gu.shape[1:]
    ff = ff2 // 2
    ft = min(MOE_FT, ff)
    nf = ff // ft
    n_sb = tables[0].shape[0]
    b_gu3 = b_gu.reshape(b_gu.shape[0], n_e, 1, ff2)
    b_d3 = b_d.reshape(b_d.shape[0], n_e, 1, d)
    max_blk = MOE_SUPER // MOE_BLK
    return pl.pallas_call(
        functools.partial(_expert_kernel, nf=nf),
        grid_spec=pltpu.PrefetchScalarGridSpec(
            num_scalar_prefetch=4,
            grid=(n_sb, nf),
            in_specs=[pl.BlockSpec(memory_space=pl.ANY),
                      pl.BlockSpec((None, None, d, ft), lambda s, f, e, r, nb, fs: (layer, e[s], 0, fs[s * nf + f])),
                      pl.BlockSpec((None, None, d, ft), lambda s, f, e, r, nb, fs: (layer, e[s], 0, nf + fs[s * nf + f])),
                      pl.BlockSpec((None, None, 1, ft), lambda s, f, e, r, nb, fs: (layer, e[s], 0, fs[s * nf + f])),
                      pl.BlockSpec((None, None, 1, ft), lambda s, f, e, r, nb, fs: (layer, e[s], 0, nf + fs[s * nf + f])),
                      pl.BlockSpec((None, None, ft, d), lambda s, f, e, r, nb, fs: (layer, e[s], fs[s * nf + f], 0)),
                      pl.BlockSpec((None, None, 1, d), lambda s, f, e, r, nb, fs: (layer, e[s], 0, 0))],
            out_specs=pl.BlockSpec(memory_space=pl.ANY),
            scratch_shapes=[pltpu.VMEM((MOE_SUPER, d), _bf16), pltpu.VMEM((MOE_SUPER, d), _f32),
                            pltpu.VMEM((MOE_STAGE, MOE_BLK, d), _f32),
                            pltpu.VMEM((d, ft), _bf16), pltpu.VMEM((d, ft), _bf16), pltpu.VMEM((ft, d), _bf16),
                            pltpu.SMEM((1,), jnp.int32),
                            pltpu.SemaphoreType.DMA((MOE_STAGE,)),
                            pltpu.SemaphoreType.DMA((max_blk,))]),
        out_shape=jax.ShapeDtypeStruct((p, d), _f32),
        input_output_aliases={4: 0},
        compiler_params=_params("arbitrary", "arbitrary"),
        name="moe_experts",
    )(*tables, xs, w_gu, w_gu, b_gu3, b_gu3, w_d, b_d3)


def _combine_kernel(dest_ref, y_ref, gate_ref, res_ref, g_ref, be_ref, o_ref, buf_ref, sem, *, tm, alpha):
    base = pl.program_id(0) * tm * TOP_K

    def issue(r, carry):
        for k in range(TOP_K):
            _row_copy(y_ref, dest_ref[base + r * TOP_K + k], buf_ref.at[k], r, sem).start()
        return carry

    def drain(r, carry):
        for k in range(TOP_K):
            _row_copy(y_ref, dest_ref[base + r * TOP_K + k], buf_ref.at[k], r, sem).wait()
        return carry

    lax.fori_loop(0, tm, issue, 0)
    lax.fori_loop(0, tm, drain, 0)
    gates = gate_ref[...]
    mix = gates[:, 0:1] * buf_ref[0]
    for k in range(1, TOP_K):
        mix = mix + gates[:, k:k + 1] * buf_ref[k]
    o_ref[...] = _ln(alpha * res_ref[...] + mix, g_ref[...], be_ref[...])


def _combine(y, dest, gates, res, g, be, *, alpha):
    n, d = res.shape
    tm = _row_tile(n, 128)
    return pl.pallas_call(
        functools.partial(_combine_kernel, tm=tm, alpha=alpha),
        grid_spec=pltpu.PrefetchScalarGridSpec(
            num_scalar_prefetch=1,
            grid=(n // tm,),
            in_specs=[pl.BlockSpec(memory_space=pl.ANY),
                      pl.BlockSpec((tm, LANE), lambda i, dest: (i, 0)),
                      pl.BlockSpec((tm, d), lambda i, dest: (i, 0)),
                      pl.BlockSpec((1, d), lambda i, dest: (0, 0)),
                      pl.BlockSpec((1, d), lambda i, dest: (0, 0))],
            out_specs=pl.BlockSpec((tm, d), lambda i, dest: (i, 0)),
            scratch_shapes=[pltpu.VMEM((TOP_K, tm, d), _f32), pltpu.SemaphoreType.DMA(())]),
        out_shape=jax.ShapeDtypeStruct((n, d), _f32),
        compiler_params=_params("arbitrary"),
        name="moe_combine",
    )(dest, y, gates, res, g, be)


def _moe_tables(idx, n_e, n_sb, nf):
    flat_e = idx[:, :TOP_K].reshape(-1)
    onehot = (flat_e[:, None] == jnp.arange(n_e, dtype=jnp.int32)[None, :]).astype(jnp.int32)
    cum = jnp.cumsum(onehot, axis=0)
    counts = cum[-1]
    rank = jnp.sum(onehot * cum, axis=1) - 1
    padded = (counts + MOE_BLK - 1) // MOE_BLK * MOE_BLK
    pad_start = jnp.cumsum(padded) - padded
    dest = (pad_start[flat_e] + rank).astype(jnp.int32)

    n_super = (padded + MOE_SUPER - 1) // MOE_SUPER
    sb_end = jnp.cumsum(n_super)
    total = sb_end[-1]
    s = jnp.arange(n_sb, dtype=jnp.int32)
    valid = s < total
    s_c = jnp.minimum(s, total - 1)
    e = jnp.minimum(jnp.searchsorted(sb_end, s_c, side='right'), n_e - 1).astype(jnp.int32)
    j = s_c - (sb_end[e] - n_super[e])
    row0 = (pad_start[e] + j * MOE_SUPER).astype(jnp.int32)
    nblk = jnp.clip((padded[e] - j * MOE_SUPER) // MOE_BLK, 0, MOE_SUPER // MOE_BLK)
    nblk = jnp.where(valid, nblk, 0).astype(jnp.int32)
    f_idx = jnp.where(valid[:, None], jnp.arange(nf, dtype=jnp.int32)[None, :], nf - 1).reshape(-1).astype(jnp.int32)
    return dest, (e, row0, nblk, f_idx)


def kernel(x_prompt, x_sample, mem_prompt, state_conv, state_hgrn, cache_swa_k, cache_swa_v, cache_mem_k, cache_mem_v, ln_g, ln_b, conv_w_in, conv_b_in, conv_w_dw, conv_b_dw, conv_ln_g, conv_ln_b, conv_w_out, conv_b_out, hgrn_w_in, hgrn_lb, hgrn_norm_g, hgrn_w_out, swa_w_qkv, swa_b_qkv, swa_sinks, swa_w_out, swa_b_out, mem_w_q, mem_w_kv, mem_w_out, moe_w_router, moe_b_router, moe_w_gate_up, moe_b_gate_up, moe_w_down, moe_b_down):
    bp, tp, d = x_prompt.shape
    bs, ts, _ = x_sample.shape
    assert bp == 1, "the prompt group is one stream"
    depth = ln_g.shape[0]
    n = tp + bs * ts
    n_e = moe_w_router.shape[-1]
    n_kv = cache_swa_k.shape[3]
    keep = cache_swa_k.shape[2]
    mem_heads = cache_mem_k.shape[3]
    m_tok = mem_prompt.shape[1]
    kw = conv_w_dw.shape[1]
    conv_state = kw - 1
    nh = d // HGRN_HEAD_DIM
    alpha = (2 * depth) ** 0.25
    grp = dict(t_prompt=tp, n_batch_s=bs, t_s=ts)
    assert ts >= conv_state and keep == WINDOW_CHUNKS * CHUNK and n_e <= LANE

    def row(v):
        return v.reshape(1, -1).astype(_f32)

    zeros_d = jnp.zeros((1, d), _f32)
    lb_p = jax.nn.softmax(hgrn_lb.astype(_f32), axis=0)
    lower_bounds = jnp.cumsum(lb_p, axis=0) - lb_p[0]

    n_assign = n * TOP_K
    p_rows = ((n_assign + n_e * (MOE_BLK - 1)) // MOE_BLK + 1) * MOE_BLK
    n_sb = (n_assign + n_e * (MOE_BLK - 1)) // MOE_SUPER + n_e
    nf = moe_w_down.shape[2] // min(MOE_FT, moe_w_down.shape[2])
    xs_buf = jnp.zeros((p_rows, d), _f32)

    x = jnp.concatenate([x_prompt.reshape(tp, d), x_sample.reshape(bs * ts, d)], axis=0)
    p_conv, s_conv, p_hgrn, s_hgrn, p_k, p_v, s_k, s_v, p_mk, p_mv = ([] for _ in range(10))

    for i in range(depth):
        kind, slot = i % 3, i // 3
        if kind == 0:
            u = _mm_glu(x, conv_w_in[slot].astype(_bf16), row(conv_b_in[slot]))
            hist = jnp.pad(state_conv[slot], ((0, 0), (HIST_ROWS - conv_state, 0), (0, 0)))
            c = _conv(u, hist, conv_w_dw[slot], row(conv_b_dw[slot]), **grp)
            p_conv.append(u[tp - conv_state:tp][None])
            s_conv.append(u[tp:].reshape(bs, ts, d)[:, ts - conv_state:])
            x = _mm_res_ln(c, conv_w_out[slot].astype(_bf16), row(conv_b_out[slot]), x,
                           row(ln_g[i, 0]), row(ln_b[i, 0]), alpha=alpha,
                           pre_norm=(row(conv_ln_g[slot]), row(conv_ln_b[slot])))
        elif kind == 1:
            proj = _mm(x, hgrn_w_in[slot].astype(_bf16), jnp.zeros((1, 4 * d), _f32))
            lbv, ngv = row(lower_bounds[i]), row(hgrn_norm_g[slot])
            s0_p = jnp.zeros((1, nh, HGRN_HEAD_DIM, HGRN_HEAD_DIM), _f32)
            s0_s = jnp.swapaxes(state_hgrn[slot].astype(_f32), -1, -2)
            o, sf_p = _hgrn(proj, lbv, ngv, s0_p, None, row_off=0, n_batch=1, t_len=tp, name="hgrn_prompt")
            o, sf_s = _hgrn(proj, lbv, ngv, s0_s, o, row_off=tp, n_batch=bs, t_len=ts, name="hgrn_sample")
            p_hgrn.append(jnp.swapaxes(sf_p, -1, -2))
            s_hgrn.append(jnp.swapaxes(sf_s, -1, -2))
            x = _mm_res_ln(o, hgrn_w_out[slot].astype(_bf16), zeros_d, x,
                           row(ln_g[i, 0]), row(ln_b[i, 0]), alpha=alpha)
        else:
            qkv = _mm(x, swa_w_qkv[slot].astype(_bf16), row(swa_b_qkv[slot]), tn=swa_w_qkv.shape[-1])
            kvw = n_kv * SWA_HEAD_DIM
            qw = qkv.shape[1] - 2 * kvw
            ck = cache_swa_k[slot].reshape(bs, keep, kvw)
            cv = cache_swa_v[slot].reshape(bs, keep, kvw)
            o = _swa(qkv, swa_sinks[slot].astype(_f32), ck, cv, n_kv=n_kv, **grp)
            k_new, v_new = qkv[:, qw:qw + kvw], qkv[:, qw + kvw:]
            p_k.append(k_new[tp - keep:tp].reshape(1, keep, n_kv, SWA_HEAD_DIM))
            p_v.append(v_new[tp - keep:tp].reshape(1, keep, n_kv, SWA_HEAD_DIM))
            s_k.append(jnp.concatenate([ck, k_new[tp:].reshape(bs, ts, kvw)], axis=1)[:, -keep:]
                       .reshape(bs, keep, n_kv, SWA_HEAD_DIM))
            s_v.append(jnp.concatenate([cv, v_new[tp:].reshape(bs, ts, kvw)], axis=1)[:, -keep:]
                       .reshape(bs, keep, n_kv, SWA_HEAD_DIM))
            x = _mm_res_ln(o, swa_w_out[slot].astype(_bf16), row(swa_b_out[slot]), x,
                           row(ln_g[i, 0]), row(ln_b[i, 0]), alpha=alpha)

        kv_p = _mm(mem_prompt.reshape(m_tok, d), mem_w_kv[i].astype(_bf16), jnp.zeros((1, 2 * d), _f32))
        p_mk.append(kv_p[:, :d].reshape(1, m_tok, mem_heads, d // mem_heads))
        p_mv.append(kv_p[:, d:].reshape(1, m_tok, mem_heads, d // mem_heads))
        q = _mm(x, mem_w_q[i].astype(_bf16), zeros_d)
        o = _mem_attend(q, kv_p, cache_mem_k, cache_mem_v, i, n_heads=mem_heads, **grp)
        x = _mm_res_ln(o, mem_w_out[i].astype(_bf16), zeros_d, x, row(ln_g[i, 1]), row(ln_b[i, 1]), alpha=alpha)

        w_r = jnp.pad(moe_w_router[i].astype(_f32), ((0, 0), (0, LANE - n_e)))
        b_r = jnp.pad(moe_b_router[i].astype(_f32), (0, LANE - n_e), constant_values=NEG).reshape(1, LANE)
        gates, idx = _router(x, w_r, b_r)
        dest, tables = _moe_tables(idx, n_e, n_sb, nf)
        xs_buf = _scatter_rows(x, dest, xs_buf)
        xs_buf = _experts(xs_buf, tables, moe_w_gate_up, moe_b_gate_up, moe_w_down, moe_b_down, i)
        x = _combine(xs_buf, dest, gates, x, row(ln_g[i, 2]), row(ln_b[i, 2]), alpha=alpha)

    y_prompt = x[:tp].reshape(1, tp, d)
    y_sample = x[tp:].reshape(bs, ts, d)
    return (y_prompt, y_sample, jnp.stack(p_conv), jnp.stack(p_hgrn), jnp.stack(p_k), jnp.stack(p_v),
            jnp.stack(p_mk), jnp.stack(p_mv), jnp.stack(s_conv), jnp.stack(s_hgrn), jnp.stack(s_k), jnp.stack(s_v))
```

```python
import functools

import jax
import jax.numpy as jnp
from jax import lax
from jax.experimental import pallas as pl
from jax.experimental.pallas import tpu as pltpu

PAST_LEN = 1024
CHUNK = 64
WINDOW_CHUNKS = 2
SWA_HEAD_DIM = 64
HGRN_HEAD_DIM = 128
TOP_K = 4
SWIGLU_LIMIT = 7.0
SWIGLU_ALPHA = 1.702
LN_EPS = 1e-5

LANE = 128
SUBLANE = 8
VMEM_LIMIT = 56 * 1024 * 1024
HIST_ROWS = 32
MOE_BLK = 256
MOE_SUPER = 1280
MOE_STAGE = 2
MOE_FT = 512
MOE_PAIR = 2
HGRN_SUB = 16
HGRN_HEADS_PER_STEP = 4
NEG = -1e30

_bf16 = jnp.bfloat16
_f32 = jnp.float32


def _params(*sem):
    return pltpu.CompilerParams(dimension_semantics=sem, vmem_limit_bytes=VMEM_LIMIT)


def _row_tile(n, cap=256):
    for t in (cap, 128, 64, 32, 16, 8):
        if t <= cap and n % t == 0:
            return t
    raise ValueError(f"row count {n} is not a multiple of {SUBLANE}")


def _tail_steps(n, n_own, tile):
    tail = n - n_own
    assert tail % tile == 0, "rows after a call's own must fill whole tiles"
    return tail // tile


def _ln(y, g, b):
    mu = jnp.mean(y, axis=-1, keepdims=True)
    d = y - mu
    var = jnp.mean(d * d, axis=-1, keepdims=True)
    return d * lax.rsqrt(var + LN_EPS) * g + b


def _dot(a, b):
    return jnp.dot(a, b, preferred_element_type=_f32)


def _dot_nt(a, b):
    return lax.dot_general(a, b, (((1,), (1,)), ((), ())), preferred_element_type=_f32)


def _mm_kernel(x_ref, w_ref, b_ref, o_ref):
    o_ref[...] = _dot(x_ref[...].astype(_bf16), w_ref[...]) + b_ref[...]


def _mm(x, w, b, *, tn=2048):
    m, k = x.shape
    n = w.shape[1]
    tn = min(tn, n)
    tm = _row_tile(m)
    return pl.pallas_call(
        _mm_kernel,
        grid=(n // tn, m // tm),
        in_specs=[pl.BlockSpec((tm, k), lambda j, i: (i, 0)),
                  pl.BlockSpec((k, tn), lambda j, i: (0, j)),
                  pl.BlockSpec((1, tn), lambda j, i: (0, j))],
        out_specs=pl.BlockSpec((tm, tn), lambda j, i: (i, j)),
        out_shape=jax.ShapeDtypeStruct((m, n), _f32),
        compiler_params=_params("arbitrary", "arbitrary"),
        name="mm_bias",
    )(x, w, b)


def _mm_glu_kernel(x_ref, wa_ref, wb_ref, ba_ref, bb_ref, o_ref):
    x = x_ref[...].astype(_bf16)
    a = _dot(x, wa_ref[...]) + ba_ref[...]
    g = _dot(x, wb_ref[...]) + bb_ref[...]
    o_ref[...] = a * jax.nn.sigmoid(g)


def _mm_glu(x, w, b, *, tn=1024):
    m, k = x.shape
    n = w.shape[1] // 2
    tn = min(tn, n)
    tm = _row_tile(m)
    nj = n // tn
    return pl.pallas_call(
        _mm_glu_kernel,
        grid=(nj, m // tm),
        in_specs=[pl.BlockSpec((tm, k), lambda j, i: (i, 0)),
                  pl.BlockSpec((k, tn), lambda j, i: (0, j)),
                  pl.BlockSpec((k, tn), lambda j, i: (0, j + nj)),
                  pl.BlockSpec((1, tn), lambda j, i: (0, j)),
                  pl.BlockSpec((1, tn), lambda j, i: (0, j + nj))],
        out_specs=pl.BlockSpec((tm, tn), lambda j, i: (i, j)),
        out_shape=jax.ShapeDtypeStruct((m, n), _f32),
        compiler_params=_params("arbitrary", "arbitrary"),
        name="mm_glu",
    )(x, w, w, b, b)


def _mm_res_ln_kernel(x_ref, w_ref, b_ref, res_ref, g_ref, be_ref, o_ref, *, alpha):
    h = _dot(x_ref[...].astype(_bf16), w_ref[...]) + b_ref[...]
    o_ref[...] = _ln(alpha * res_ref[...] + h, g_ref[...], be_ref[...])


def _mm_norm_res_ln_kernel(x_ref, ng_ref, nb_ref, w_ref, b_ref, res_ref, g_ref, be_ref, o_ref, *, alpha):
    c = _ln(x_ref[...], ng_ref[...], nb_ref[...])
    c = c * jax.nn.sigmoid(c)
    h = _dot(c.astype(_bf16), w_ref[...]) + b_ref[...]
    o_ref[...] = _ln(alpha * res_ref[...] + h, g_ref[...], be_ref[...])


def _mm_res_ln(x, w, b, res, g, be, *, alpha, pre_norm=None):
    m, k = x.shape
    n = w.shape[1]
    tm = _row_tile(m)
    row = lambda i: (i, 0)
    fix = lambda i: (0, 0)
    specs = [pl.BlockSpec((tm, k), row)]
    args = [x]
    if pre_norm is not None:
        specs += [pl.BlockSpec((1, k), fix), pl.BlockSpec((1, k), fix)]
        args += list(pre_norm)
        body = _mm_norm_res_ln_kernel
    else:
        body = _mm_res_ln_kernel
    specs += [pl.BlockSpec((k, n), fix), pl.BlockSpec((1, n), fix), pl.BlockSpec((tm, n), row),
              pl.BlockSpec((1, n), fix), pl.BlockSpec((1, n), fix)]
    args += [w, b, res, g, be]
    return pl.pallas_call(
        functools.partial(body, alpha=alpha),
        grid=(m // tm,),
        in_specs=specs,
        out_specs=pl.BlockSpec((tm, n), row),
        out_shape=jax.ShapeDtypeStruct((m, n), _f32),
        compiler_params=_params("arbitrary"),
        name="mm_res_ln",
    )(*args)


def _conv_kernel(hist_ref, u_ref, w_ref, b_ref, o_ref, buf_ref, shift_ref, *, n_real, tt, lb, kw):
    i = pl.program_id(0)

    def compute():
        hist = hist_ref[...]
        if n_real is not None:
            hist = jnp.where(i == 0, 0.0, hist)
        buf_ref[0:HIST_ROWS, :] = hist
        buf_ref[HIST_ROWS:, :] = u_ref[...]
        off = HIST_ROWS - (kw - 1)
        rb = min(tt, 64)
        for r in range(tt // rb):
            for c in range(lb // LANE):
                cs = slice(c * LANE, (c + 1) * LANE)
                acc = jnp.zeros((rb, LANE), _f32) + b_ref[:, cs]
                for s in range(SUBLANE):
                    taps = [p - off for p in range(s, off + kw, SUBLANE) if p >= off]
                    if not taps:
                        continue
                    rows = rb if s == 0 else rb + SUBLANE
                    v = None
                    for j in taps:
                        a0 = r * rb + off + j - s
                        term = w_ref[j:j + 1, cs] * buf_ref[a0:a0 + rows, cs]
                        v = term if v is None else v + term
                    if s == 0:
                        acc = acc + v
                    else:
                        shift_ref[s, 0:rows, :] = v
                        acc = acc + shift_ref[s, s:s + rb, :]
                o_ref[r * rb:(r + 1) * rb, cs] = acc

    if n_real is None:
        compute()
    else:
        pl.when(i < n_real)(compute)

        @pl.when(i >= n_real)
        def _():
            o_ref[...] = jnp.zeros(o_ref.shape, _f32)


def _conv(u, hist_s, w, b, *, t_prompt, n_batch_s, t_s):
    n, d = u.shape
    kw = w.shape[0]
    lb = min(512, d)
    tt = _row_tile(t_prompt)
    hb = tt // HIST_ROWS
    nt = t_prompt // tt
    last = nt - 1
    common = dict(out_shape=jax.ShapeDtypeStruct((n, d), _f32), compiler_params=_params("arbitrary", "arbitrary"))
    c = pl.pallas_call(
        functools.partial(_conv_kernel, n_real=nt, tt=tt, lb=lb, kw=kw),
        grid=(nt + _tail_steps(n, t_prompt, tt), d // lb),
        in_specs=[pl.BlockSpec((HIST_ROWS, lb), lambda i, j: (jnp.maximum(jnp.minimum(i, last) * hb - 1, 0), j)),
                  pl.BlockSpec((tt, lb), lambda i, j: (jnp.minimum(i, last), j)),
                  pl.BlockSpec((kw, lb), lambda i, j: (0, j)),
                  pl.BlockSpec((1, lb), lambda i, j: (0, j))],
        out_specs=pl.BlockSpec((tt, lb), lambda i, j: (i, j)),
        scratch_shapes=[pltpu.VMEM((HIST_ROWS + tt, lb), _f32),
                        pltpu.VMEM((SUBLANE, min(tt, 64) + SUBLANE, LANE), _f32)],
        name="conv_prompt", **common,
    )(u, u, w, b)
    ob = t_prompt // t_s
    return pl.pallas_call(
        lambda full_ref, *refs: _conv_kernel(*refs, n_real=None, tt=t_s, lb=lb, kw=kw),
        grid=(n_batch_s, d // lb),
        in_specs=[pl.BlockSpec(memory_space=pl.ANY),
                  pl.BlockSpec((None, HIST_ROWS, lb), lambda i, j: (i, 0, j)),
                  pl.BlockSpec((t_s, lb), lambda i, j: (ob + i, j)),
                  pl.BlockSpec((kw, lb), lambda i, j: (0, j)),
                  pl.BlockSpec((1, lb), lambda i, j: (0, j))],
        out_specs=pl.BlockSpec((t_s, lb), lambda i, j: (ob + i, j)),
        scratch_shapes=[pltpu.VMEM((HIST_ROWS + t_s, lb), _f32),
                        pltpu.VMEM((SUBLANE, min(t_s, 64) + SUBLANE, LANE), _f32)],
        input_output_aliases={0: 0},
        name="conv_sample", **common,
    )(c, hist_s, u, w, b)


def _hgrn_kernel(q_ref, f_ref, i_ref, og_ref, lb_ref, ng_ref, s0_ref, o_ref, sf_ref, st_ref, *,
                 tt, blk, heads, n_real):
    t = pl.program_id(2)
    hd = HGRN_HEAD_DIM

    @pl.when(t == 0)
    def _():
        st_ref[...] = s0_ref[...]

    rr = lax.broadcasted_iota(jnp.int32, (blk, blk), 0)
    cc = lax.broadcasted_iota(jnp.int32, (blk, blk), 1)
    tril = (cc <= rr).astype(_f32)
    sub_row = lax.broadcasted_iota(jnp.int32, (HGRN_SUB, LANE), 0)

    def head_chunk(base, hh):
        rows = pl.ds(base, blk)
        ls = slice(hh * hd, (hh + 1) * hd)
        lbv = lb_ref[:, ls]
        q = q_ref[rows, ls]
        qh = q * jax.nn.sigmoid(q)
        forget = lbv + (1.0 - lbv) * jax.nn.sigmoid(f_ref[rows, ls])
        kh = 1.0 - forget
        v = i_ref[rows, ls]
        b = jnp.dot(tril, jnp.log(forget), precision=lax.Precision.HIGHEST, preferred_element_type=_f32)
        st = st_ref[hh]
        inter = _dot_nt((qh * jnp.exp(b)).astype(_bf16), st.astype(_bf16))
        b_last = b[blk - 1:blk, :]
        kf_end = (kh * jnp.exp(b_last - b)).astype(_bf16)
        st_ref[hh] = st * jnp.exp(b_last) + lax.dot_general(
            v.astype(_bf16), kf_end, (((0,), (0,)), ((), ())), preferred_element_type=_f32)
        vb = v.astype(_bf16)
        for blk_i in range(blk // HGRN_SUB):
            r0 = blk_i * HGRN_SUB
            b_i = b[r0:r0 + HGRN_SUB, :]
            q_i = qh[r0:r0 + HGRN_SUB, :]
            acc = inter[r0:r0 + HGRN_SUB, :]
            for s in range(HGRN_SUB):
                keep = sub_row >= s
                decay = jnp.exp(jnp.where(keep, b_i - b[r0 + s:r0 + s + 1, :], NEG))
                col = jnp.sum(q_i * decay * kh[r0 + s:r0 + s + 1, :], axis=-1, keepdims=True)
                acc = acc + col * v[r0 + s:r0 + s + 1, :]
            if r0 > 0:
                anchor = b[r0 - 1:r0, :]
                qf = (q_i * jnp.exp(b_i - anchor)).astype(_bf16)
                kf = (kh[0:r0, :] * jnp.exp(anchor - b[0:r0, :])).astype(_bf16)
                acc = acc + _dot(_dot_nt(qf, kf).astype(_bf16), vb[0:r0, :])
            o = acc * lax.rsqrt(jnp.mean(acc * acc, axis=-1, keepdims=True) + LN_EPS)
            o_rows = pl.ds(base + r0, HGRN_SUB)
            o_ref[o_rows, ls] = o * ng_ref[:, ls] * jax.nn.sigmoid(og_ref[o_rows, ls])

    @pl.when(t < n_real)
    def _():
        for c in range(tt // blk):
            for hh in range(heads):
                head_chunk(c * blk, hh)

    @pl.when(t >= n_real)
    def _():
        o_ref[...] = jnp.zeros(o_ref.shape, _f32)

    @pl.when(t == n_real - 1)
    def _():
        sf_ref[...] = st_ref[...]


def _hgrn(proj, lb, ng, s0_t, o_prev, *, row_off, n_batch, t_len, name):
    n, d4 = proj.shape
    d = d4 // 4
    hd = HGRN_HEAD_DIM
    nh = d // hd
    heads = HGRN_HEADS_PER_STEP if nh % HGRN_HEADS_PER_STEP == 0 else 1
    n_groups = nh // heads
    w = heads * hd
    blk = min(t_len, CHUNK)
    tt = _row_tile(t_len)
    nt = t_len // tt
    last = nt - 1
    ob = row_off // tt
    tail = _tail_steps(n, row_off + n_batch * t_len, tt) if o_prev is None else 0
    row = lambda k: (lambda b, h, t: (ob + b * nt + jnp.minimum(t, last), k * n_groups + h))
    head = lambda b, h, t: (0, h)
    in_specs = [pl.BlockSpec((tt, w), row(0)), pl.BlockSpec((tt, w), row(1)),
                pl.BlockSpec((tt, w), row(2)), pl.BlockSpec((tt, w), row(3)),
                pl.BlockSpec((1, w), head), pl.BlockSpec((1, w), head),
                pl.BlockSpec((None, heads, hd, hd), lambda b, h, t: (b, h, 0, 0))]
    args = [proj, proj, proj, proj, lb, ng, s0_t]
    body = functools.partial(_hgrn_kernel, tt=tt, blk=blk, heads=heads, n_real=nt)
    aliases = {}
    if o_prev is not None:
        in_specs = [pl.BlockSpec(memory_space=pl.ANY)] + in_specs
        args = [o_prev] + args
        inner = body
        body = lambda full_ref, *refs: inner(*refs)
        aliases = {0: 0}
    return pl.pallas_call(
        body,
        grid=(n_batch, n_groups, nt + tail),
        in_specs=in_specs,
        out_specs=[pl.BlockSpec((tt, w), lambda b, h, t: (ob + b * nt + t, h)),
                   pl.BlockSpec((None, heads, hd, hd), lambda b, h, t: (b, h, 0, 0))],
        out_shape=[jax.ShapeDtypeStruct((n, d), _f32),
                   jax.ShapeDtypeStruct((n_batch, nh, hd, hd), _f32)],
        scratch_shapes=[pltpu.VMEM((heads, hd, hd), _f32)],
        input_output_aliases=aliases,
        compiler_params=_params("arbitrary", "arbitrary", "arbitrary"),
        name=name,
    )(*args)


def _swa_kernel(sink_ref, q_ref, kc_ref, vc_ref, kp_ref, vp_ref, o_ref, *, tq, win, pos0, n_kv, group, n_real):
    dh = SWA_HEAD_DIM
    i = pl.program_id(1)
    n_heads = n_kv * group

    def compute():
        qpos0 = pos0 + i * tq

        def bias_terms(nk, kpos0):
            qp = qpos0 + lax.broadcasted_iota(jnp.int32, (tq, nk), 0)
            kp = kpos0 + lax.broadcasted_iota(jnp.int32, (tq, nk), 1)
            qc = qp // CHUNK
            kc = jnp.maximum(kp, 0) // CHUNK
            vis = (kp >= 0) & (kc <= qc) & (qc - kc <= WINDOW_CHUNKS)
            return vis, jnp.abs(qp - kp).astype(_f32)

        vis_p, dist_p = bias_terms(win, qpos0 - win)
        vis_c, dist_c = bias_terms(tq, qpos0)
        scale = dh ** -0.5
        for kv in range(n_kv):
            ks = slice(kv * dh, (kv + 1) * dh)
            kp = kp_ref[:, ks].astype(_bf16)
            kc = kc_ref[:, ks].astype(_bf16)
            vp = vp_ref[:, ks].astype(_bf16)
            vc = vc_ref[:, ks].astype(_bf16)
            heads = [kv * group + g for g in range(group)]
            qs = jnp.concatenate([q_ref[:, h * dh:(h + 1) * dh] for h in heads], axis=0).astype(_bf16)
            raw_p = _dot_nt(qs, kp)
            raw_c = _dot_nt(qs, kc)
            w_p, w_c = [], []
            for g, h in enumerate(heads):
                slope = 2.0 ** (-8.0 * (h + 1) / n_heads)
                rs = slice(g * tq, (g + 1) * tq)
                s_p = jnp.where(vis_p, raw_p[rs] * scale - slope * dist_p, NEG)
                s_c = jnp.where(vis_c, raw_c[rs] * scale - slope * dist_c, NEG)
                sink = sink_ref[h]
                m = jnp.maximum(jnp.maximum(jnp.max(s_p, axis=-1, keepdims=True),
                                            jnp.max(s_c, axis=-1, keepdims=True)), sink)
                e_p = jnp.exp(s_p - m)
                e_c = jnp.exp(s_c - m)
                den = (jnp.sum(e_p, axis=-1, keepdims=True) + jnp.sum(e_c, axis=-1, keepdims=True)
                       + jnp.exp(sink - m))
                inv = 1.0 / den
                w_p.append((e_p * inv).astype(_bf16))
                w_c.append((e_c * inv).astype(_bf16))
            o = _dot(jnp.concatenate(w_p, axis=0), vp) + _dot(jnp.concatenate(w_c, axis=0), vc)
            for g, h in enumerate(heads):
                o_ref[:, h * dh:(h + 1) * dh] = o[g * tq:(g + 1) * tq]

    if n_real is None:
        compute()
    else:
        pl.when(i < n_real)(compute)

        @pl.when(i >= n_real)
        def _():
            o_ref[...] = jnp.zeros(o_ref.shape, _f32)


def _swa(qkv, sinks, cache_k, cache_v, *, t_prompt, n_batch_s, t_s, n_kv):
    n, width = qkv.shape
    dh = SWA_HEAD_DIM
    kvw = n_kv * dh
    qw = width - 2 * kvw
    group = qw // kvw
    win = WINDOW_CHUNKS * CHUNK
    kcol, vcol = qw // kvw, qw // kvw + 1
    common = dict(out_shape=jax.ShapeDtypeStruct((n, qw), _f32), compiler_params=_params("arbitrary", "arbitrary"))
    smem = pl.BlockSpec(memory_space=pltpu.SMEM)
    tq = win
    nt = t_prompt // tq
    last = nt - 1
    cur = lambda c: (lambda b, i: (jnp.minimum(i, last), c))
    prev = lambda c: (lambda b, i: (jnp.maximum(jnp.minimum(i, last) - 1, 0), c))
    o = pl.pallas_call(
        functools.partial(_swa_kernel, tq=tq, win=win, pos0=0, n_kv=n_kv, group=group, n_real=nt),
        grid=(1, nt + _tail_steps(n, t_prompt, tq)),
        in_specs=[smem,
                  pl.BlockSpec((tq, qw), cur(0)),
                  pl.BlockSpec((tq, kvw), cur(kcol)),
                  pl.BlockSpec((tq, kvw), cur(vcol)),
                  pl.BlockSpec((win, kvw), prev(kcol)),
                  pl.BlockSpec((win, kvw), prev(vcol))],
        out_specs=pl.BlockSpec((tq, qw), lambda b, i: (i, 0)),
        name="swa_prompt", **common,
    )(sinks, qkv, qkv, qkv, qkv, qkv)
    ob = t_prompt // t_s
    inner = functools.partial(_swa_kernel, tq=t_s, win=win, pos0=PAST_LEN, n_kv=n_kv, group=group, n_real=None)
    return pl.pallas_call(
        lambda full_ref, *refs: inner(*refs),
        grid=(n_batch_s, 1),
        in_specs=[pl.BlockSpec(memory_space=pl.ANY), smem,
                  pl.BlockSpec((t_s, qw), lambda b, i: (ob + b, 0)),
                  pl.BlockSpec((t_s, kvw), lambda b, i: (ob + b, kcol)),
                  pl.BlockSpec((t_s, kvw), lambda b, i: (ob + b, vcol)),
                  pl.BlockSpec((None, win, kvw), lambda b, i: (b, 0, 0)),
                  pl.BlockSpec((None, win, kvw), lambda b, i: (b, 0, 0))],
        out_specs=pl.BlockSpec((t_s, qw), lambda b, i: (ob + b, 0)),
        input_output_aliases={0: 0},
        name="swa_sample", **common,
    )(o, sinks, qkv, qkv, qkv, cache_k, cache_v)


def _mem_kernel(q_ref, k_ref, v_ref, o_ref, *, scale, n_real):
    def compute():
        s = _dot_nt(q_ref[...].astype(_bf16), k_ref[...].astype(_bf16)) * scale
        e = jnp.exp(s - jnp.max(s, axis=-1, keepdims=True))
        w = e * (1.0 / jnp.sum(e, axis=-1, keepdims=True))
        o_ref[...] = _dot(w.astype(_bf16), v_ref[...].astype(_bf16))

    if n_real is None:
        compute()
    else:
        i = pl.program_id(1)
        pl.when(i < n_real)(compute)

        @pl.when(i >= n_real)
        def _():
            o_ref[...] = jnp.zeros(o_ref.shape, _f32)


def _mem_attend(q, kv_p, k_s, v_s, layer, *, t_prompt, n_batch_s, t_s, n_heads):
    n, d = q.shape
    dh = d // n_heads
    m = kv_p.shape[0]
    common = dict(out_shape=jax.ShapeDtypeStruct((n, d), _f32), compiler_params=_params("arbitrary", "arbitrary"))
    tq = _row_tile(t_prompt)
    nt = t_prompt // tq
    last = nt - 1
    o = pl.pallas_call(
        functools.partial(_mem_kernel, scale=dh ** -0.5, n_real=nt),
        grid=(n_heads, nt + _tail_steps(n, t_prompt, tq)),
        in_specs=[pl.BlockSpec((tq, dh), lambda h, i: (jnp.minimum(i, last), h)),
                  pl.BlockSpec((m, dh), lambda h, i: (0, h)),
                  pl.BlockSpec((m, dh), lambda h, i: (0, n_heads + h))],
        out_specs=pl.BlockSpec((tq, dh), lambda h, i: (i, h)),
        name="mem_prompt", **common,
    )(q, kv_p, kv_p)
    ob = t_prompt // t_s

    def sample_body(full_ref, q_ref, k_ref, v_ref, o_ref):
        del full_ref
        for h in range(n_heads):
            hs = slice(h * dh, (h + 1) * dh)
            s = _dot_nt(q_ref[:, hs].astype(_bf16), k_ref[:, h, :].astype(_bf16)) * dh ** -0.5
            e = jnp.exp(s - jnp.max(s, axis=-1, keepdims=True))
            w = e * (1.0 / jnp.sum(e, axis=-1, keepdims=True))
            o_ref[:, hs] = _dot(w.astype(_bf16), v_ref[:, h, :].astype(_bf16))

    cache_spec = pl.BlockSpec((None, None, m, n_heads, dh), lambda b: (layer, b, 0, 0, 0))
    return pl.pallas_call(
        sample_body,
        grid=(n_batch_s,),
        in_specs=[pl.BlockSpec(memory_space=pl.ANY),
                  pl.BlockSpec((t_s, d), lambda b: (ob + b, 0)),
                  cache_spec, cache_spec],
        out_specs=pl.BlockSpec((t_s, d), lambda b: (ob + b, 0)),
        out_shape=jax.ShapeDtypeStruct((n, d), _f32),
        input_output_aliases={0: 0},
        compiler_params=_params("arbitrary"),
        name="mem_sample",
    )(o, q, k_s, v_s)


def _router_kernel(x_ref, w_ref, b_ref, gate_ref, idx_ref):
    logits = jnp.dot(x_ref[...], w_ref[...], precision=lax.Precision.HIGHEST,
                     preferred_element_type=_f32) + b_ref[...]
    lane = lax.broadcasted_iota(jnp.int32, logits.shape, 1)
    lane_f = lane.astype(_f32)
    vals, ids = [], []
    for _ in range(TOP_K):
        m = jnp.max(logits, axis=-1, keepdims=True)
        am = jnp.min(jnp.where(logits == m, lane_f, float(LANE)), axis=-1, keepdims=True)
        vals.append(m)
        ids.append(am)
        logits = jnp.where(lane_f == am, -jnp.inf, logits)
    es = [jnp.exp(v - vals[0]) for v in vals]
    inv = 1.0 / sum(es)
    gates = jnp.zeros(logits.shape, _f32)
    idx = jnp.zeros(logits.shape, _f32)
    for k in range(TOP_K):
        gates = jnp.where(lane == k, es[k] * inv, gates)
        idx = jnp.where(lane == k, ids[k], idx)
    gate_ref[...] = gates
    idx_ref[...] = idx.astype(jnp.int32)


def _router(x, w_pad, b_pad):
    n, d = x.shape
    tm = _row_tile(n)
    return pl.pallas_call(
        _router_kernel,
        grid=(n // tm,),
        in_specs=[pl.BlockSpec((tm, d), lambda i: (i, 0)),
                  pl.BlockSpec((d, LANE), lambda i: (0, 0)),
                  pl.BlockSpec((1, LANE), lambda i: (0, 0))],
        out_specs=[pl.BlockSpec((tm, LANE), lambda i: (i, 0)), pl.BlockSpec((tm, LANE), lambda i: (i, 0))],
        out_shape=[jax.ShapeDtypeStruct((n, LANE), _f32), jax.ShapeDtypeStruct((n, LANE), jnp.int32)],
        compiler_params=_params("arbitrary"),
        name="router",
    )(x, w_pad, b_pad)


def _row_copy(src, s, dst, d, sem):
    return pltpu.make_async_copy(src.at[pl.ds(s, 1), :], dst.at[pl.ds(d, 1), :], sem)


def _scatter_kernel(dest_ref, x_ref, xs_in_ref, xs_ref, sem, *, tm):
    del xs_in_ref
    base = pl.program_id(0) * tm * TOP_K

    def issue(r, carry):
        for k in range(TOP_K):
            _row_copy(x_ref, r, xs_ref, dest_ref[base + r * TOP_K + k], sem).start()
        return carry

    def drain(r, carry):
        for k in range(TOP_K):
            _row_copy(x_ref, r, xs_ref, dest_ref[base + r * TOP_K + k], sem).wait()
        return carry

    lax.fori_loop(0, tm, issue, 0)
    lax.fori_loop(0, tm, drain, 0)


def _scatter_rows(x, dest, xs_buf):
    n, d = x.shape
    tm = _row_tile(n, 128)
    return pl.pallas_call(
        functools.partial(_scatter_kernel, tm=tm),
        grid_spec=pltpu.PrefetchScalarGridSpec(
            num_scalar_prefetch=1,
            grid=(n // tm,),
            in_specs=[pl.BlockSpec((tm, d), lambda i, dest: (i, 0)),
                      pl.BlockSpec(memory_space=pl.ANY)],
            out_specs=pl.BlockSpec(memory_space=pl.ANY),
            scratch_shapes=[pltpu.SemaphoreType.DMA(())]),
        out_shape=jax.ShapeDtypeStruct(xs_buf.shape, _f32),
        input_output_aliases={2: 0},
        compiler_params=_params("arbitrary"),
        name="moe_scatter",
    )(dest, x, xs_buf)


def _expert_kernel(sb_e_ref, sb_row_ref, sb_nblk_ref, sb_f_ref, xs_ref, wg_ref, wl_ref, bg_ref, bl_ref,
                   wd_ref, bd_ref, out_ref, xb_ref, acc_ref, stage_ref, wgb_ref, wlb_ref, wdb_ref,
                   pend_ref, sem_in, sem_out, *, nf):
    del sb_e_ref, sb_f_ref
    s = pl.program_id(0)
    f = pl.program_id(1)
    nblk = sb_nblk_ref[s]
    row0 = pl.multiple_of(sb_row_ref[s], MOE_BLK)
    max_blk = xb_ref.shape[0] // MOE_BLK
    n_stage = stage_ref.shape[0]

    def rows_of(j):
        return pl.ds(j * MOE_BLK, MOE_BLK)

    def in_copy(j):
        return pltpu.make_async_copy(xs_ref.at[pl.ds(row0 + j * MOE_BLK, MOE_BLK), :],
                                     stage_ref.at[j % n_stage], sem_in.at[j % n_stage])

    def out_copy(j):
        return pltpu.make_async_copy(acc_ref.at[rows_of(j), :],
                                     out_ref.at[pl.ds(row0 + j * MOE_BLK, MOE_BLK), :], sem_out.at[j])

    def drain_out():
        pending = pend_ref[0]
        for j in range(max_blk):
            @pl.when(j < pending)
            def _():
                out_copy(j).wait()
        pend_ref[0] = 0

    @pl.when((s == 0) & (f == 0))
    def _():
        pend_ref[0] = 0

    @pl.when((f == 0) & (nblk > 0))
    def _():
        for j in range(min(n_stage, max_blk)):
            @pl.when(j < nblk)
            def _():
                in_copy(j).start()
        for j in range(max_blk):
            @pl.when(j < nblk)
            def _():
                in_copy(j).wait()
                xb_ref[rows_of(j), :] = stage_ref[j % n_stage].astype(_bf16)
                if j + n_stage < max_blk:
                    @pl.when(j + n_stage < nblk)
                    def _():
                        in_copy(j + n_stage).start()
        drain_out()

    @pl.when(nblk > 0)
    def _():
        wgb_ref[...] = wg_ref[...].astype(_bf16)
        wlb_ref[...] = wl_ref[...].astype(_bf16)
        wdb_ref[...] = wd_ref[...].astype(_bf16)

        def rows_mlp(rows, first):
            x = xb_ref[rows, :]
            gate = jnp.minimum(_dot(x, wgb_ref[...]) + bg_ref[...], SWIGLU_LIMIT)
            lin = jnp.clip(_dot(x, wlb_ref[...]) + bl_ref[...], -SWIGLU_LIMIT, SWIGLU_LIMIT)
            act = gate * jax.nn.sigmoid(SWIGLU_ALPHA * gate) * (lin + 1.0)
            part = _dot(act.astype(_bf16), wdb_ref[...])
            if first:
                acc_ref[rows, :] = part + bd_ref[...]
            else:
                acc_ref[rows, :] += part

        def sweep(first):
            group = MOE_PAIR * MOE_BLK
            n_group = nblk // MOE_PAIR

            def trip(j, carry):
                rows_mlp(pl.ds(pl.multiple_of(j * group, group), group), first)
                return carry

            lax.fori_loop(0, n_group, trip, 0)
            for extra in range(MOE_PAIR - 1):
                @pl.when(n_group * MOE_PAIR + extra < nblk)
                def _():
                    start = pl.multiple_of((n_group * MOE_PAIR + extra) * MOE_BLK, MOE_BLK)
                    rows_mlp(pl.ds(start, MOE_BLK), first)

        @pl.when(f == 0)
        def _():
            sweep(True)

        @pl.when(f > 0)
        def _():
            sweep(False)

    @pl.when((f == nf - 1) & (nblk > 0))
    def _():
        for j in range(max_blk):
            @pl.when(j < nblk)
            def _():
                out_copy(j).start()
        pend_ref[0] = nblk

    @pl.when((s == pl.num_programs(0) - 1) & (f == nf - 1))
    def _():
        drain_out()


def _experts(xs, tables, w_gu, b_gu, w_d, b_d, layer):
    p, d = xs.shape
    n_e, _, ff2 = w_gu.shape[1:]
    ff = ff2 // 2
    ft = min(MOE_FT, ff)
    nf = ff // ft
    n_sb = tables[0].shape[0]
    b_gu3 = b_gu.reshape(b_gu.shape[0], n_e, 1, ff2)
    b_d3 = b_d.reshape(b_d.shape[0], n_e, 1, d)
    max_blk = MOE_SUPER // MOE_BLK
    return pl.pallas_call(
        functools.partial(_expert_kernel, nf=nf),
        grid_spec=pltpu.PrefetchScalarGridSpec(
            num_scalar_prefetch=4,
            grid=(n_sb, nf),
            in_specs=[pl.BlockSpec(memory_space=pl.ANY),
                      pl.BlockSpec((None, None, d, ft), lambda s, f, e, r, nb, fs: (layer, e[s], 0, fs[s * nf + f])),
                      pl.BlockSpec((None, None, d, ft), lambda s, f, e, r, nb, fs: (layer, e[s], 0, nf + fs[s * nf + f])),
                      pl.BlockSpec((None, None, 1, ft), lambda s, f, e, r, nb, fs: (layer, e[s], 0, fs[s * nf + f])),
                      pl.BlockSpec((None, None, 1, ft), lambda s, f, e, r, nb, fs: (layer, e[s], 0, nf + fs[s * nf + f])),
                      pl.BlockSpec((None, None, ft, d), lambda s, f, e, r, nb, fs: (layer, e[s], fs[s * nf + f], 0)),
                      pl.BlockSpec((None, None, 1, d), lambda s, f, e, r, nb, fs: (layer, e[s], 0, 0))],
            out_specs=pl.BlockSpec(memory_space=pl.ANY),
            scratch_shapes=[pltpu.VMEM((MOE_SUPER, d), _bf16), pltpu.VMEM((MOE_SUPER, d), _f32),
                            pltpu.VMEM((MOE_STAGE, MOE_BLK, d), _f32),
                            pltpu.VMEM((d, ft), _bf16), pltpu.VMEM((d, ft), _bf16), pltpu.VMEM((ft, d), _bf16),
                            pltpu.SMEM((1,), jnp.int32),
                            pltpu.SemaphoreType.DMA((MOE_STAGE,)),
                            pltpu.SemaphoreType.DMA((max_blk,))]),
        out_shape=jax.ShapeDtypeStruct((p, d), _f32),
        input_output_aliases={4: 0},
        compiler_params=_params("arbitrary", "arbitrary"),
        name="moe_experts",
    )(*tables, xs, w_gu, w_gu, b_gu3, b_gu3, w_d, b_d3)


def _combine_kernel(dest_ref, y_ref, gate_ref, res_ref, g_ref, be_ref, o_ref, buf_ref, sem, *, tm, alpha):
    base = pl.program_id(0) * tm * TOP_K

    def issue(r, carry):
        for k in range(TOP_K):
            _row_copy(y_ref, dest_ref[base + r * TOP_K + k], buf_ref.at[k], r, sem).start()
        return carry

    def drain(r, carry):
        for k in range(TOP_K):
            _row_copy(y_ref, dest_ref[base + r * TOP_K + k], buf_ref.at[k], r, sem).wait()
        return carry

    lax.fori_loop(0, tm, issue, 0)
    lax.fori_loop(0, tm, drain, 0)
    gates = gate_ref[...]
    mix = gates[:, 0:1] * buf_ref[0]
    for k in range(1, TOP_K):
        mix = mix + gates[:, k:k + 1] * buf_ref[k]
    o_ref[...] = _ln(alpha * res_ref[...] + mix, g_ref[...], be_ref[...])


def _combine(y, dest, gates, res, g, be, *, alpha):
    n, d = res.shape
    tm = _row_tile(n, 128)
    return pl.pallas_call(
        functools.partial(_combine_kernel, tm=tm, alpha=alpha),
        grid_spec=pltpu.PrefetchScalarGridSpec(
            num_scalar_prefetch=1,
            grid=(n // tm,),
            in_specs=[pl.BlockSpec(memory_space=pl.ANY),
                      pl.BlockSpec((tm, LANE), lambda i, dest: (i, 0)),
                      pl.BlockSpec((tm, d), lambda i, dest: (i, 0)),
                      pl.BlockSpec((1, d), lambda i, dest: (0, 0)),
                      pl.BlockSpec((1, d), lambda i, dest: (0, 0))],
            out_specs=pl.BlockSpec((tm, d), lambda i, dest: (i, 0)),
            scratch_shapes=[pltpu.VMEM((TOP_K, tm, d), _f32), pltpu.SemaphoreType.DMA(())]),
        out_shape=jax.ShapeDtypeStruct((n, d), _f32),
        compiler_params=_params("arbitrary"),
        name="moe_combine",
    )(dest, y, gates, res, g, be)


def _moe_tables(idx, n_e, n_sb, nf):
    flat_e = idx[:, :TOP_K].reshape(-1)
    onehot = (flat_e[:, None] == jnp.arange(n_e, dtype=jnp.int32)[None, :]).astype(jnp.int32)
    cum = jnp.cumsum(onehot, axis=0)
    counts = cum[-1]
    rank = jnp.sum(onehot * cum, axis=1) - 1
    padded = (counts + MOE_BLK - 1) // MOE_BLK * MOE_BLK
    pad_start = jnp.cumsum(padded) - padded
    dest = (pad_start[flat_e] + rank).astype(jnp.int32)

    n_super = (padded + MOE_SUPER - 1) // MOE_SUPER
    sb_end = jnp.cumsum(n_super)
    total = sb_end[-1]
    s = jnp.arange(n_sb, dtype=jnp.int32)
    valid = s < total
    s_c = jnp.minimum(s, total - 1)
    e = jnp.minimum(jnp.searchsorted(sb_end, s_c, side='right'), n_e - 1).astype(jnp.int32)
    j = s_c - (sb_end[e] - n_super[e])
    row0 = (pad_start[e] + j * MOE_SUPER).astype(jnp.int32)
    nblk = jnp.clip((padded[e] - j * MOE_SUPER) // MOE_BLK, 0, MOE_SUPER // MOE_BLK)
    nblk = jnp.where(valid, nblk, 0).astype(jnp.int32)
    f_idx = jnp.where(valid[:, None], jnp.arange(nf, dtype=jnp.int32)[None, :], nf - 1).reshape(-1).astype(jnp.int32)
    return dest, (e, row0, nblk, f_idx)


def kernel(x_prompt, x_sample, mem_prompt, state_conv, state_hgrn, cache_swa_k, cache_swa_v, cache_mem_k, cache_mem_v, ln_g, ln_b, conv_w_in, conv_b_in, conv_w_dw, conv_b_dw, conv_ln_g, conv_ln_b, conv_w_out, conv_b_out, hgrn_w_in, hgrn_lb, hgrn_norm_g, hgrn_w_out, swa_w_qkv, swa_b_qkv, swa_sinks, swa_w_out, swa_b_out, mem_w_q, mem_w_kv, mem_w_out, moe_w_router, moe_b_router, moe_w_gate_up, moe_b_gate_up, moe_w_down, moe_b_down):
    bp, tp, d = x_prompt.shape
    bs, ts, _ = x_sample.shape
    assert bp == 1, "the prompt group is one stream"
    depth = ln_g.shape[0]
    n = tp + bs * ts
    n_e = moe_w_router.shape[-1]
    n_kv = cache_swa_k.shape[3]
    keep = cache_swa_k.shape[2]
    mem_heads = cache_mem_k.shape[3]
    m_tok = mem_prompt.shape[1]
    kw = conv_w_dw.shape[1]
    conv_state = kw - 1
    nh = d // HGRN_HEAD_DIM
    alpha = (2 * depth) ** 0.25
    grp = dict(t_prompt=tp, n_batch_s=bs, t_s=ts)
    assert ts >= conv_state and keep == WINDOW_CHUNKS * CHUNK and n_e <= LANE

    def row(v):
        return v.reshape(1, -1).astype(_f32)

    zeros_d = jnp.zeros((1, d), _f32)
    lb_p = jax.nn.softmax(hgrn_lb.astype(_f32), axis=0)
    lower_bounds = jnp.cumsum(lb_p, axis=0) - lb_p[0]

    n_assign = n * TOP_K
    p_rows = ((n_assign + n_e * (MOE_BLK - 1)) // MOE_BLK + 1) * MOE_BLK
    n_sb = (n_assign + n_e * (MOE_BLK - 1)) // MOE_SUPER + n_e
    nf = moe_w_down.shape[2] // min(MOE_FT, moe_w_down.shape[2])
    xs_buf = jnp.zeros((p_rows, d), _f32)

    x = jnp.concatenate([x_prompt.reshape(tp, d), x_sample.reshape(bs * ts, d)], axis=0)
    p_conv, s_conv, p_hgrn, s_hgrn, p_k, p_v, s_k, s_v, p_mk, p_mv = ([] for _ in range(10))

    for i in range(depth):
        kind, slot = i % 3, i // 3
        if kind == 0:
            u = _mm_glu(x, conv_w_in[slot].astype(_bf16), row(conv_b_in[slot]))
            hist = jnp.pad(state_conv[slot], ((0, 0), (HIST_ROWS - conv_state, 0), (0, 0)))
            c = _conv(u, hist, conv_w_dw[slot], row(conv_b_dw[slot]), **grp)
            p_conv.append(u[tp - conv_state:tp][None])
            s_conv.append(u[tp:].reshape(bs, ts, d)[:, ts - conv_state:])
            x = _mm_res_ln(c, conv_w_out[slot].astype(_bf16), row(conv_b_out[slot]), x,
                           row(ln_g[i, 0]), row(ln_b[i, 0]), alpha=alpha,
                           pre_norm=(row(conv_ln_g[slot]), row(conv_ln_b[slot])))
        elif kind == 1:
            proj = _mm(x, hgrn_w_in[slot].astype(_bf16), jnp.zeros((1, 4 * d), _f32))
            lbv, ngv = row(lower_bounds[i]), row(hgrn_norm_g[slot])
            s0_p = jnp.zeros((1, nh, HGRN_HEAD_DIM, HGRN_HEAD_DIM), _f32)
            s0_s = jnp.swapaxes(state_hgrn[slot].astype(_f32), -1, -2)
            o, sf_p = _hgrn(proj, lbv, ngv, s0_p, None, row_off=0, n_batch=1, t_len=tp, name="hgrn_prompt")
            o, sf_s = _hgrn(proj, lbv, ngv, s0_s, o, row_off=tp, n_batch=bs, t_len=ts, name="hgrn_sample")
            p_hgrn.append(jnp.swapaxes(sf_p, -1, -2))
            s_hgrn.append(jnp.swapaxes(sf_s, -1, -2))
            x = _mm_res_ln(o, hgrn_w_out[slot].astype(_bf16), zeros_d, x,
                           row(ln_g[i, 0]), row(ln_b[i, 0]), alpha=alpha)
        else:
            qkv = _mm(x, swa_w_qkv[slot].astype(_bf16), row(swa_b_qkv[slot]), tn=swa_w_qkv.shape[-1])
            kvw = n_kv * SWA_HEAD_DIM
            qw = qkv.shape[1] - 2 * kvw
            ck = cache_swa_k[slot].reshape(bs, keep, kvw)
            cv = cache_swa_v[slot].reshape(bs, keep, kvw)
            o = _swa(qkv, swa_sinks[slot].astype(_f32), ck, cv, n_kv=n_kv, **grp)
            k_new, v_new = qkv[:, qw:qw + kvw], qkv[:, qw + kvw:]
            p_k.append(k_new[tp - keep:tp].reshape(1, keep, n_kv, SWA_HEAD_DIM))
            p_v.append(v_new[tp - keep:tp].reshape(1, keep, n_kv, SWA_HEAD_DIM))
            s_k.append(jnp.concatenate([ck, k_new[tp:].reshape(bs, ts, kvw)], axis=1)[:, -keep:]
                       .reshape(bs, keep, n_kv, SWA_HEAD_DIM))
            s_v.append(jnp.concatenate([cv, v_new[tp:].reshape(bs, ts, kvw)], axis=1)[:, -keep:]
                       .reshape(bs, keep, n_kv, SWA_HEAD_DIM))
            x = _mm_res_ln(o, swa_w_out[slot].astype(_bf16), row(swa_b_out[slot]), x,
                           row(ln_g[i, 0]), row(ln_b[i, 0]), alpha=alpha)

        kv_p = _mm(mem_prompt.reshape(m_tok, d), mem_w_kv[i].astype(_bf16), jnp.zeros((1, 2 * d), _f32))
        p_mk.append(kv_p[:, :d].reshape(1, m_tok, mem_heads, d // mem_heads))
        p_mv.append(kv_p[:, d:].reshape(1, m_tok, mem_heads, d // mem_heads))
        q = _mm(x, mem_w_q[i].astype(_bf16), zeros_d)
        o = _mem_attend(q, kv_p, cache_mem_k, cache_mem_v, i, n_heads=mem_heads, **grp)
        x = _mm_res_ln(o, mem_w_out[i].astype(_bf16), zeros_d, x, row(ln_g[i, 1]), row(ln_b[i, 1]), alpha=alpha)

        w_r = jnp.pad(moe_w_router[i].astype(_f32), ((0, 0), (0, LANE - n_e)))
        b_r = jnp.pad(moe_b_router[i].astype(_f32), (0, LANE - n_e), constant_values=NEG).reshape(1, LANE)
        gates, idx = _router(x, w_r, b_r)
        dest, tables = _moe_tables(idx, n_e, n_sb, nf)
        xs_buf = _scatter_rows(x, dest, xs_buf)
        xs_buf = _experts(xs_buf, tables, moe_w_gate_up, moe_b_gate_up, moe_w_down, moe_b_down, i)
        x = _combine(xs_buf, dest, gates, x, row(ln_g[i, 2]), row(ln_b[i, 2]), alpha=alpha)

    y_prompt = x[:tp].reshape(1, tp, d)
    y_sample = x[tp:].reshape(bs, ts, d)
    return (y_prompt, y_sample, jnp.stack(p_conv), jnp.stack(p_hgrn), jnp.stack(p_k), jnp.stack(p_v),
            jnp.stack(p_mk), jnp.stack(p_mv), jnp.stack(s_conv), jnp.stack(s_hgrn), jnp.stack(s_k), jnp.stack(s_v))
```
